```python
import math
import jax, jax.numpy as jnp
from jax import lax
import numpy as np

D_MODEL = 1024
BATCH = 1
SEQ = 16384
DEPTH = 1

MIX_WIDTH = D_MODEL
ATTN_WIDTH = MIX_WIDTH // 2
CONV_WIDTH = MIX_WIDTH - ATTN_WIDTH
DIFF_HEAD_DIM = 64
V_HEAD_DIM = 2 * DIFF_HEAD_DIM
N_ATTN_HEADS = ATTN_WIDTH // V_HEAD_DIM
CONV_KERNEL = 31
FFN_HIDDEN = ((-(-8 * D_MODEL // 3) + 255) // 256) * 256
PLE_DIM = 256
Q_BLOCK = 128
IN_COLS = 3 * ATTN_WIDTH + 2 * CONV_WIDTH
EPS = 1e-6

kernel_name = "hybrid_diffattn_conformer_conv_block"


def alibi_slopes(n_heads):
    return np.array([2.0 ** (-8.0 * (h + 1) / n_heads) for h in range(n_heads)], dtype=np.float32)


def rmsnorm(x, gain):
    x32 = x.astype(jnp.float32)
    y = x32 * lax.rsqrt(jnp.mean(x32 * x32, axis=-1, keepdims=True) + EPS)
    return (y * gain.astype(jnp.float32)).astype(x.dtype)


def layernorm(x, gain, bias):
    x32 = x.astype(jnp.float32)
    mu = jnp.mean(x32, axis=-1, keepdims=True)
    xc = x32 - mu
    y = xc * lax.rsqrt(jnp.mean(xc * xc, axis=-1, keepdims=True) + EPS)
    return (y * gain.astype(jnp.float32) + bias.astype(jnp.float32)).astype(x.dtype)


def diff_attention(q, k, v, lam):
    B, S = q.shape[0], q.shape[1]
    nblk = S // Q_BLOCK
    scale = DIFF_HEAD_DIM ** -0.5
    slopes = jnp.asarray(alibi_slopes(N_ATTN_HEADS))
    key_pos = jnp.arange(S, dtype=jnp.int32)
    q_blocks = jnp.moveaxis(q.reshape(B, nblk, Q_BLOCK, N_ATTN_HEADS, 2, DIFF_HEAD_DIM), 1, 0)
    starts = jnp.arange(nblk, dtype=jnp.int32) * Q_BLOCK

    def block(args):
        q_blk, start = args
        s = jnp.einsum('bqhmd,bkhmd->bhmqk', q_blk, k).astype(jnp.float32) * scale
        q_pos = start + jnp.arange(Q_BLOCK, dtype=jnp.int32)
        dist = jnp.abs(q_pos[:, None] - key_pos[None, :]).astype(jnp.float32)
        s = s - slopes[:, None, None, None] * dist
        a = jax.nn.softmax(s, axis=-1)
        w = a[:, :, 0] - lam * a[:, :, 1]
        return jnp.einsum('bhqk,bkhe->bqhe', w.astype(v.dtype), v)

    o = lax.map(block, (q_blocks, starts))
    return jnp.moveaxis(o, 0, 1).reshape(B, S, N_ATTN_HEADS, V_HEAD_DIM)


def conformer_conv(a, g, conv_w, conv_b, ln_g, ln_b):
    u = a * jax.nn.sigmoid(g)
    rhs = conv_w[:, None, :].astype(u.dtype)
    y = lax.conv_general_dilated(
        u, rhs, window_strides=(1,),
        padding=[(CONV_KERNEL // 2, CONV_KERNEL // 2)],
        dimension_numbers=('NWC', 'WIO', 'NWC'),
        feature_group_count=CONV_WIDTH) + conv_b.astype(u.dtype)
    y = layernorm(y, ln_g, ln_b)
    return jax.nn.silu(y)


def setup_inputs(seed: int = 0) -> dict:
    key = jax.random.key(seed)
    ks = jax.random.split(key, 24)
    f32 = jnp.float32

    def nrm(k, shape, scale):
        return jax.random.normal(k, shape, f32) * scale

    def gain(k, shape):
        return 1.0 + 0.05 * jax.random.normal(k, shape, f32)

    L = DEPTH
    return {
        "x": nrm(ks[0], (BATCH, SEQ, D_MODEL), 1.0),
        "p": nrm(ks[1], (DEPTH, BATCH, SEQ, PLE_DIM), 1.0),
        "attn_norm": gain(ks[2], (L, D_MODEL)),
        "w_in": nrm(ks[3], (L, D_MODEL, IN_COLS), D_MODEL ** -0.5),
        "q_norm": gain(ks[4], (L, DIFF_HEAD_DIM)),
        "k_norm": gain(ks[5], (L, DIFF_HEAD_DIM)),
        "lambda_q1": nrm(ks[6], (L, DIFF_HEAD_DIM), 0.1),
        "lambda_k1": nrm(ks[7], (L, DIFF_HEAD_DIM), 0.1),
        "lambda_q2": nrm(ks[8], (L, DIFF_HEAD_DIM), 0.1),
        "lambda_k2": nrm(ks[9], (L, DIFF_HEAD_DIM), 0.1),
        "head_norm": gain(ks[10], (L, V_HEAD_DIM)),
        "conv_w": nrm(ks[11], (L, CONV_KERNEL, CONV_WIDTH), CONV_KERNEL ** -0.5),
        "conv_b": nrm(ks[12], (L, CONV_WIDTH), 0.02),
        "conv_ln_g": gain(ks[13], (L, CONV_WIDTH)),
        "conv_ln_b": nrm(ks[14], (L, CONV_WIDTH), 0.02),
        "w_out": nrm(ks[15], (L, MIX_WIDTH, D_MODEL), MIX_WIDTH ** -0.5),
        "ffn_norm": gain(ks[16], (L, D_MODEL)),
        "w_gate": nrm(ks[17], (L, D_MODEL, FFN_HIDDEN), D_MODEL ** -0.5),
        "w_up": nrm(ks[18], (L, D_MODEL, FFN_HIDDEN), D_MODEL ** -0.5),
        "w_down": nrm(ks[19], (L, FFN_HIDDEN, D_MODEL), FFN_HIDDEN ** -0.5),
        "ple_norm": gain(ks[20], (L, D_MODEL)),
        "w_ple_gate": nrm(ks[21], (L, D_MODEL, D_MODEL), D_MODEL ** -0.5),
        "w_ple_proj": nrm(ks[22], (L, PLE_DIM, D_MODEL), PLE_DIM ** -0.5),
    }


def reference(x, p, attn_norm, w_in, q_norm, k_norm, lambda_q1, lambda_k1, lambda_q2, lambda_k2,
              head_norm, conv_w, conv_b, conv_ln_g, conv_ln_b, w_out, ffn_norm, w_gate, w_up,
              w_down, ple_norm, w_ple_gate, w_ple_proj):
    B, S = x.shape[0], x.shape[1]
    splits = [ATTN_WIDTH, 2 * ATTN_WIDTH, 3 * ATTN_WIDTH, 3 * ATTN_WIDTH + CONV_WIDTH]
    for i in range(DEPTH):
        h = rmsnorm(x, attn_norm[i])
        proj = h @ w_in[i]
        q, k, v, conv_a, conv_g = jnp.split(proj, splits, axis=-1)

        q = rmsnorm(q.reshape(B, S, N_ATTN_HEADS, 2, DIFF_HEAD_DIM), q_norm[i])
        k = rmsnorm(k.reshape(B, S, N_ATTN_HEADS, 2, DIFF_HEAD_DIM), k_norm[i])
        v = v.reshape(B, S, N_ATTN_HEADS, V_HEAD_DIM)
        lam_init = 0.8 - 0.6 * math.exp(-0.3 * i)
        lam = (jnp.exp(jnp.sum(lambda_q1[i].astype(jnp.float32) * lambda_k1[i].astype(jnp.float32)))
               - jnp.exp(jnp.sum(lambda_q2[i].astype(jnp.float32) * lambda_k2[i].astype(jnp.float32)))
               + lam_init)
        o = diff_attention(q, k, v, lam)
        o = rmsnorm(o, head_norm[i]) * (1.0 - lam_init)
        attn_out = o.reshape(B, S, ATTN_WIDTH)

        conv_out = conformer_conv(conv_a, conv_g, conv_w[i], conv_b[i], conv_ln_g[i], conv_ln_b[i])

        mix = jnp.concatenate([attn_out, conv_out], axis=-1)
        x = x + mix @ w_out[i]

        h = rmsnorm(x, ffn_norm[i])
        x = x + (jax.nn.silu(h @ w_gate[i]) * (h @ w_up[i])) @ w_down[i]

        gate = jax.nn.sigmoid(rmsnorm(x, ple_norm[i]) @ w_ple_gate[i])
        x = x + gate * (p[i] @ w_ple_proj[i])
    return x
```

```python
import functools
import math

import numpy as np
import jax
import jax.numpy as jnp
from jax import lax
from jax.experimental import pallas as pl
from jax.experimental.pallas import tpu as pltpu

D_MODEL = 1024
ATTN_WIDTH = 512
CONV_WIDTH = 512
DIFF_HEAD_DIM = 64
V_HEAD_DIM = 128
N_ATTN_HEADS = 4
CONV_KERNEL = 31
CONV_HALO = 16
FFN_HIDDEN = 2816
PLE_DIM = 256
EPS = 1e-6

VMEM_LIMIT_BYTES = 56 * 1024 * 1024

F32 = jnp.float32
BF16 = jnp.bfloat16
NEG_BIG = -1e30


def _rmsnorm_rows(x, gain):
    ms = jnp.mean(x * x, axis=-1, keepdims=True)
    return x * lax.rsqrt(ms + EPS) * gain


def _dot(a, b):
    return jnp.dot(a, b, preferred_element_type=F32)


def _dot_nt(a, b):
    return lax.dot_general(a, b, (((1,), (1,)), ((), ())), preferred_element_type=F32)


def _inproj_kernel(x_ref, g_ref, wqk_ref, wvt_ref, wc_ref, qn_ref, kn_ref, seg_ref,
                   q_ref, k_ref, vt_ref, u_ref):
    h = _rmsnorm_rows(x_ref[...], g_ref[...]).astype(BF16)
    qk = _dot(h, wqk_ref[...])

    def seg_norm(y, gain):
        sq = y * y
        hi = sq.astype(BF16)
        lo = (sq - hi.astype(F32)).astype(BF16)
        ms = _dot(hi, seg_ref[...]) + _dot(lo, seg_ref[...])
        return y * lax.rsqrt(ms + EPS) * gain

    q = seg_norm(qk[:, :ATTN_WIDTH], qn_ref[...]) * (DIFF_HEAD_DIM ** -0.5)
    k = seg_norm(qk[:, ATTN_WIDTH:], kn_ref[...])
    q_ref[...] = q.astype(BF16)
    k_ref[...] = k.astype(BF16)
    vt_ref[...] = _dot_nt(wvt_ref[...], h).astype(BF16)
    c = _dot(h, wc_ref[...])
    u_ref[...] = c[:, :CONV_WIDTH] * jax.nn.sigmoid(c[:, CONV_WIDTH:])


def _inproj(x, gain, wqk, wvt, wc, qn, kn, seg, *, tm):
    s = x.shape[0]
    const = lambda i: (0, 0)
    return pl.pallas_call(
        _inproj_kernel,
        grid=(s // tm,),
        in_specs=[
            pl.BlockSpec((tm, D_MODEL), lambda i: (i, 0)),
            pl.BlockSpec((1, D_MODEL), const),
            pl.BlockSpec((D_MODEL, 2 * ATTN_WIDTH), const),
            pl.BlockSpec((ATTN_WIDTH, D_MODEL), const),
            pl.BlockSpec((D_MODEL, 2 * CONV_WIDTH), const),
            pl.BlockSpec((1, ATTN_WIDTH), const),
            pl.BlockSpec((1, ATTN_WIDTH), const),
            pl.BlockSpec((ATTN_WIDTH, ATTN_WIDTH), const),
        ],
        out_specs=[
            pl.BlockSpec((tm, ATTN_WIDTH), lambda i: (i, 0)),
            pl.BlockSpec((tm, ATTN_WIDTH), lambda i: (i, 0)),
            pl.BlockSpec((ATTN_WIDTH, tm), lambda i: (0, i)),
            pl.BlockSpec((tm, CONV_WIDTH), lambda i: (i, 0)),
        ],
        out_shape=[
            jax.ShapeDtypeStruct((s, ATTN_WIDTH), BF16),
            jax.ShapeDtypeStruct((s, ATTN_WIDTH), BF16),
            jax.ShapeDtypeStruct((ATTN_WIDTH, s), BF16),
            jax.ShapeDtypeStruct((s, CONV_WIDTH), F32),
        ],
        compiler_params=pltpu.CompilerParams(
            dimension_semantics=("arbitrary",), vmem_limit_bytes=VMEM_LIMIT_BYTES),
        name="inproj",
    )(x, gain, wqk, wvt, wc, qn, kn, seg)


def _attn_kernel(slopes_ref, q_ref, k_ref, vt_ref, lq1_ref, lk1_ref, lq2_ref, lk2_ref, hn_ref,
                 o_ref, acc1_ref, acc2_ref, *, tq, tk, seq, lam_init):
    head = pl.program_id(0)
    qi = pl.program_id(1)
    slope = slopes_ref[head]

    q = q_ref[...]
    lane = lax.broadcasted_iota(jnp.int32, q.shape, 1)
    zero = jnp.zeros_like(q)
    qz = jnp.concatenate([jnp.where(lane < DIFF_HEAD_DIM, q, zero),
                          jnp.where(lane >= DIFF_HEAD_DIM, q, zero)], axis=0)

    rel = (lax.broadcasted_iota(jnp.int32, (tk, tq), 0)
           - lax.broadcasted_iota(jnp.int32, (tk, tq), 1)).astype(F32)

    acc1_ref[...] = jnp.zeros_like(acc1_ref)
    acc2_ref[...] = jnp.zeros_like(acc2_ref)

    def update(s, m, l, acc_ref, vtc):
        m_new = jnp.maximum(m, jnp.max(s, axis=0, keepdims=True))
        alpha = jnp.exp(m - m_new)
        p = jnp.exp(s - m_new)
        l_new = alpha * l + jnp.sum(p, axis=0, keepdims=True)
        acc_ref[...] = alpha * acc_ref[...] + _dot(vtc, p.astype(BF16))
        return m_new, l_new

    def body(j, carry):
        m1, l1, m2, l2 = carry
        k0 = pl.multiple_of(j * tk, tk)
        kc = k_ref[pl.ds(k0, tk), :]
        vtc = vt_ref[:, pl.ds(k0, tk)]
        s = _dot_nt(kc, qz)
        off = (j * tk - qi * tq).astype(F32)
        bias = slope * jnp.abs(rel + off)
        m1, l1 = update(s[:, :tq] - bias, m1, l1, acc1_ref, vtc)
        m2, l2 = update(s[:, tq:] - bias, m2, l2, acc2_ref, vtc)
        return m1, l1, m2, l2

    init = (jnp.full((1, tq), NEG_BIG, F32), jnp.zeros((1, tq), F32),
            jnp.full((1, tq), NEG_BIG, F32), jnp.zeros((1, tq), F32))
    m1, l1, m2, l2 = lax.fori_loop(0, seq // tk, body, init)

    lam = (jnp.exp(jnp.sum(lq1_ref[...] * lk1_ref[...], axis=-1, keepdims=True))
           - jnp.exp(jnp.sum(lq2_ref[...] * lk2_ref[...], axis=-1, keepdims=True))
           + lam_init)
    o_t = acc1_ref[...] / l1 - lam * (acc2_ref[...] / l2)
    o = o_t.T
    o_ref[...] = (_rmsnorm_rows(o, hn_ref[...]) * (1.0 - lam_init)).astype(o_ref.dtype)


def _attention(slopes, q, k, vt, lq1, lk1, lq2, lk2, hn, *, tq, tk, lam_init):
    s = q.shape[0]
    lam_spec = pl.BlockSpec((1, DIFF_HEAD_DIM), lambda h, i: (0, 0))
    kern = functools.partial(_attn_kernel, tq=tq, tk=tk, seq=s, lam_init=lam_init)
    return pl.pallas_call(
        kern,
        grid=(N_ATTN_HEADS, s // tq),
        in_specs=[
            pl.BlockSpec(memory_space=pltpu.SMEM),
            pl.BlockSpec((tq, V_HEAD_DIM), lambda h, i: (i, h)),
            pl.BlockSpec((s, V_HEAD_DIM), lambda h, i: (0, h)),
            pl.BlockSpec((V_HEAD_DIM, s), lambda h, i: (h, 0)),
            lam_spec, lam_spec, lam_spec, lam_spec,
            pl.BlockSpec((1, V_HEAD_DIM), lambda h, i: (0, 0)),
        ],
        out_specs=pl.BlockSpec((tq, V_HEAD_DIM), lambda h, i: (i, h)),
        out_shape=jax.ShapeDtypeStruct((s, ATTN_WIDTH), BF16),
        scratch_shapes=[pltpu.VMEM((V_HEAD_DIM, tq), F32), pltpu.VMEM((V_HEAD_DIM, tq), F32)],
        compiler_params=pltpu.CompilerParams(
            dimension_semantics=("arbitrary", "arbitrary"), vmem_limit_bytes=VMEM_LIMIT_BYTES),
        name="diff_attention",
    )(slopes, q, k, vt, lq1, lk1, lq2, lk2, hn)


def _conv_kernel(prev_ref, main_ref, next_ref, w_ref, b_ref, g_ref, beta_ref, o_ref, buf_ref,
                 *, ts, rows):
    i = pl.program_id(0)
    last = pl.num_programs(0) - 1
    buf_ref[0:CONV_HALO, :] = jnp.where(i > 0, prev_ref[...], 0.0)
    buf_ref[CONV_HALO:CONV_HALO + ts, :] = main_ref[...]
    buf_ref[CONV_HALO + ts:2 * CONV_HALO + ts, :] = jnp.where(i < last, next_ref[...], 0.0)
    first_tap = CONV_HALO - CONV_KERNEL // 2
    for r0 in range(0, ts, rows):
        acc = jnp.broadcast_to(b_ref[...], (rows, CONV_WIDTH))
        for t in range(CONV_KERNEL):
            start = r0 + first_tap + t
            acc = acc + w_ref[t:t + 1, :] * buf_ref[start:start + rows, :]
        mu = jnp.mean(acc, axis=-1, keepdims=True)
        xc = acc - mu
        y = xc * lax.rsqrt(jnp.mean(xc * xc, axis=-1, keepdims=True) + EPS)
        y = y * g_ref[...] + beta_ref[...]
        o_ref[r0:r0 + rows, :] = (y * jax.nn.sigmoid(y)).astype(o_ref.dtype)


def _conformer_conv(u, w, b, g, beta, *, ts, rows):
    s = u.shape[0]
    nh = ts // CONV_HALO
    n_halo_blocks = s // CONV_HALO
    const = lambda i: (0, 0)
    kern = functools.partial(_conv_kernel, ts=ts, rows=rows)
    return pl.pallas_call(
        kern,
        grid=(s // ts,),
        in_specs=[
            pl.BlockSpec((CONV_HALO, CONV_WIDTH), lambda i: (jnp.maximum(i * nh - 1, 0), 0)),
            pl.BlockSpec((ts, CONV_WIDTH), lambda i: (i, 0)),
            pl.BlockSpec((CONV_HALO, CONV_WIDTH),
                         lambda i: (jnp.minimum((i + 1) * nh, n_halo_blocks - 1), 0)),
            pl.BlockSpec((CONV_KERNEL, CONV_WIDTH), const),
            pl.BlockSpec((1, CONV_WIDTH), const),
            pl.BlockSpec((1, CONV_WIDTH), const),
            pl.BlockSpec((1, CONV_WIDTH), const),
        ],
        out_specs=pl.BlockSpec((ts, CONV_WIDTH), lambda i: (i, 0)),
        out_shape=jax.ShapeDtypeStruct((s, CONV_WIDTH), BF16),
        scratch_shapes=[pltpu.VMEM((ts + 2 * CONV_HALO, CONV_WIDTH), F32)],
        compiler_params=pltpu.CompilerParams(
            dimension_semantics=("arbitrary",), vmem_limit_bytes=VMEM_LIMIT_BYTES),
        name="conformer_conv",
    )(u, u, u, w, b, g, beta)


FFN_CHUNKS = ((0, 1024), (1024, 2048), (2048, FFN_HIDDEN))


def _tail_kernel(x_ref, a_ref, c_ref, p_ref, woa_ref, woc_ref, fg_ref, wg_ref, wu_ref, wd_ref,
                 pg_ref, wpg_ref, wpp_ref, o_ref, act_ref):
    x1 = x_ref[...] + _dot(a_ref[...], woa_ref[...]) + _dot(c_ref[...], woc_ref[...])
    h = _rmsnorm_rows(x1, fg_ref[...]).astype(BF16)
    for c0, c1 in FFN_CHUNKS:
        g = _dot(h, wg_ref[:, c0:c1])
        u = _dot(h, wu_ref[:, c0:c1])
        act_ref[:, c0:c1] = (g * jax.nn.sigmoid(g) * u).astype(BF16)
    x2 = x1 + _dot(act_ref[...], wd_ref[...])
    hp = _rmsnorm_rows(x2, pg_ref[...]).astype(BF16)
    gate = jax.nn.sigmoid(_dot(hp, wpg_ref[...]))
    o_ref[...] = x2 + gate * _dot(p_ref[...].astype(BF16), wpp_ref[...])


def _tail(x, a, c, p, woa, woc, fg, wg, wu, wd, pg, wpg, wpp, *, tm):
    s = x.shape[0]
    const = lambda i: (0, 0)

    def resident(shape):
        return pl.BlockSpec(shape, const, pipeline_mode=pl.Buffered(1))

    return pl.pallas_call(
        _tail_kernel,
        grid=(s // tm,),
        in_specs=[
            pl.BlockSpec((tm, D_MODEL), lambda i: (i, 0)),
            pl.BlockSpec((tm, ATTN_WIDTH), lambda i: (i, 0)),
            pl.BlockSpec((tm, CONV_WIDTH), lambda i: (i, 0)),
            pl.BlockSpec((tm, PLE_DIM), lambda i: (i, 0)),
            resident((ATTN_WIDTH, D_MODEL)),
            resident((CONV_WIDTH, D_MODEL)),
            resident((1, D_MODEL)),
            resident((D_MODEL, FFN_HIDDEN)),
            resident((D_MODEL, FFN_HIDDEN)),
            resident((FFN_HIDDEN, D_MODEL)),
            resident((1, D_MODEL)),
            resident((D_MODEL, D_MODEL)),
            resident((PLE_DIM, D_MODEL)),
        ],
        out_specs=pl.BlockSpec((tm, D_MODEL), lambda i: (i, 0)),
        out_shape=jax.ShapeDtypeStruct((s, D_MODEL), F32),
        scratch_shapes=[pltpu.VMEM((tm, FFN_HIDDEN), BF16)],
        compiler_params=pltpu.CompilerParams(
            dimension_semantics=("arbitrary",), vmem_limit_bytes=VMEM_LIMIT_BYTES),
        name="tail",
    )(x, a, c, p, woa, woc, fg, wg, wu, wd, pg, wpg, wpp)


def _layer(x, p, lam_init, attn_norm, w_in, q_norm, k_norm, lq1, lk1, lq2, lk2, head_norm,
           conv_w, conv_b, conv_ln_g, conv_ln_b, w_out, ffn_norm, w_gate, w_up, w_down,
           ple_norm, w_ple_gate, w_ple_proj):
    row = lambda v: v.reshape(1, -1).astype(F32)
    a = ATTN_WIDTH
    wqk = w_in[:, :2 * a].astype(BF16)
    wvt = w_in[:, 2 * a:3 * a].T.astype(BF16)
    wc = w_in[:, 3 * a:].astype(BF16)
    n_seg = ATTN_WIDTH // DIFF_HEAD_DIM
    seg = jnp.asarray(np.kron(np.eye(n_seg), np.full((DIFF_HEAD_DIM, DIFF_HEAD_DIM),
                                                      1.0 / DIFF_HEAD_DIM)), BF16)
    qn = row(jnp.tile(q_norm, n_seg))
    kn = row(jnp.tile(k_norm, n_seg))
    q, k, vt, u = _inproj(x, row(attn_norm), wqk, wvt, wc, qn, kn, seg, tm=512)

    slopes = jnp.asarray([2.0 ** (-8.0 * (h + 1) / N_ATTN_HEADS) for h in range(N_ATTN_HEADS)], F32)
    attn = _attention(slopes, q, k, vt, row(lq1), row(lk1), row(lq2), row(lk2), row(head_norm),
                      tq=256, tk=512, lam_init=lam_init)
    conv = _conformer_conv(u, conv_w.astype(F32), row(conv_b), row(conv_ln_g), row(conv_ln_b),
                           ts=512, rows=64)
    return _tail(x, attn, conv, p, w_out[:a].astype(BF16), w_out[a:].astype(BF16), row(ffn_norm),
                 w_gate.astype(BF16), w_up.astype(BF16), w_down.astype(BF16), row(ple_norm),
                 w_ple_gate.astype(BF16), w_ple_proj.astype(BF16), tm=512)


def kernel(x, p, attn_norm, w_in, q_norm, k_norm, lambda_q1, lambda_k1, lambda_q2, lambda_k2,
           head_norm, conv_w, conv_b, conv_ln_g, conv_ln_b, w_out, ffn_norm, w_gate, w_up, w_down,
           ple_norm, w_ple_gate, w_ple_proj):
    depth = w_in.shape[0]
    batch = x.shape[0]
    outs = []
    for b in range(batch):
        xb = x[b]
        for i in range(depth):
            lam_init = 0.8 - 0.6 * math.exp(-0.3 * i)
            xb = _layer(xb, p[i, b], lam_init, attn_norm[i], w_in[i], q_norm[i], k_norm[i],
                        lambda_q1[i], lambda_k1[i], lambda_q2[i], lambda_k2[i], head_norm[i],
                        conv_w[i], conv_b[i], conv_ln_g[i], conv_ln_b[i], w_out[i], ffn_norm[i],
                        w_gate[i], w_up[i], w_down[i], ple_norm[i], w_ple_gate[i], w_ple_proj[i])
        outs.append(xb)
    return jnp.stack(outs, axis=0)
```

```python
import functools
import math

import numpy as np
import jax
import jax.numpy as jnp
from jax import lax
from jax.experimental import pallas as pl
from jax.experimental.pallas import tpu as pltpu

D_MODEL = 1024
ATTN_WIDTH = 512
CONV_WIDTH = 512
DIFF_HEAD_DIM = 64
V_HEAD_DIM = 128
N_ATTN_HEADS = 4
CONV_KERNEL = 31
CONV_HALO = 16
FFN_HIDDEN = 2816
PLE_DIM = 256
EPS = 1e-6

VMEM_LIMIT_BYTES = 56 * 1024 * 1024

F32 = jnp.float32
BF16 = jnp.bfloat16
NEG_BIG = -1e30
LOG2E = math.log2(math.e)


def _rmsnorm_rows(x, gain):
    ms = jnp.mean(x * x, axis=-1, keepdims=True)
    return x * lax.rsqrt(ms + EPS) * gain


def _dot(a, b):
    return jnp.dot(a, b, preferred_element_type=F32)


def _dot_nt(a, b):
    return lax.dot_general(a, b, (((1,), (1,)), ((), ())), preferred_element_type=F32)


def _inproj_kernel(x_ref, g_ref, wqk_ref, wvt_ref, wc_ref, qn_ref, kn_ref, seg_ref,
                   q_ref, k_ref, vt_ref, u_ref):
    h = _rmsnorm_rows(x_ref[...], g_ref[...]).astype(BF16)
    qk = _dot(h, wqk_ref[...])

    def seg_norm(y, gain):
        sq = y * y
        hi = sq.astype(BF16)
        lo = (sq - hi.astype(F32)).astype(BF16)
        ms = _dot(hi, seg_ref[...]) + _dot(lo, seg_ref[...])
        return y * lax.rsqrt(ms + EPS) * gain

    q = seg_norm(qk[:, :ATTN_WIDTH], qn_ref[...]) * (DIFF_HEAD_DIM ** -0.5 * LOG2E)
    k = seg_norm(qk[:, ATTN_WIDTH:], kn_ref[...])
    q_ref[...] = q.astype(BF16)
    k_ref[...] = k.astype(BF16)
    vt_ref[...] = _dot_nt(wvt_ref[...], h).astype(BF16)
    c = _dot(h, wc_ref[...])
    u_ref[...] = c[:, :CONV_WIDTH] * jax.nn.sigmoid(c[:, CONV_WIDTH:])


def _inproj(x, gain, wqk, wvt, wc, qn, kn, seg, *, tm):
    s = x.shape[0]
    const = lambda i: (0, 0)
    return pl.pallas_call(
        _inproj_kernel,
        grid=(s // tm,),
        in_specs=[
            pl.BlockSpec((tm, D_MODEL), lambda i: (i, 0)),
            pl.BlockSpec((1, D_MODEL), const),
            pl.BlockSpec((D_MODEL, 2 * ATTN_WIDTH), const),
            pl.BlockSpec((ATTN_WIDTH, D_MODEL), const),
            pl.BlockSpec((D_MODEL, 2 * CONV_WIDTH), const),
            pl.BlockSpec((1, ATTN_WIDTH), const),
            pl.BlockSpec((1, ATTN_WIDTH), const),
            pl.BlockSpec((ATTN_WIDTH, ATTN_WIDTH), const),
        ],
        out_specs=[
            pl.BlockSpec((tm, ATTN_WIDTH), lambda i: (i, 0)),
            pl.BlockSpec((tm, ATTN_WIDTH), lambda i: (i, 0)),
            pl.BlockSpec((ATTN_WIDTH, tm), lambda i: (0, i)),
            pl.BlockSpec((tm, CONV_WIDTH), lambda i: (i, 0)),
        ],
        out_shape=[
            jax.ShapeDtypeStruct((s, ATTN_WIDTH), BF16),
            jax.ShapeDtypeStruct((s, ATTN_WIDTH), BF16),
            jax.ShapeDtypeStruct((ATTN_WIDTH, s), BF16),
            jax.ShapeDtypeStruct((s, CONV_WIDTH), F32),
        ],
        compiler_params=pltpu.CompilerParams(
            dimension_semantics=("arbitrary",), vmem_limit_bytes=VMEM_LIMIT_BYTES),
        name="inproj",
    )(x, gain, wqk, wvt, wc, qn, kn, seg)


N_POS_SPLIT = 3
POS_HI_STEP = 16


def _bf16_pieces(value, n):
    pieces, rest = [], np.float64(value)
    for _ in range(n):
        piece = np.float64(np.asarray(rest, dtype=BF16).astype(np.float64))
        pieces.append(piece)
        rest = rest - piece
    return pieces


def _alibi_tables(tq, tk):
    assert tk // POS_HI_STEP <= 256 and tq <= 256
    pos = np.arange(tk)
    feat = np.zeros((N_ATTN_HEADS, tk, V_HEAD_DIM), np.float64)
    coef = np.zeros((N_ATTN_HEADS, 1, V_HEAD_DIM), np.float64)
    base = np.zeros((tq, V_HEAD_DIM), np.float64)
    n_lin = 2 * N_POS_SPLIT
    base[:, :n_lin] = 1.0
    base[:, n_lin:n_lin + N_POS_SPLIT] = np.arange(tq)[:, None]
    for h in range(N_ATTN_HEADS):
        c = 2.0 ** (-8.0 * (h + 1) / N_ATTN_HEADS) * LOG2E
        for s, piece in enumerate(_bf16_pieces(c, N_POS_SPLIT)):
            feat[h, :, 2 * s] = pos // POS_HI_STEP
            feat[h, :, 2 * s + 1] = pos % POS_HI_STEP
            feat[h, :, n_lin + s] = piece
            coef[h, 0, 2 * s] = -POS_HI_STEP * piece
            coef[h, 0, 2 * s + 1] = -piece
        coef[h, 0, n_lin:n_lin + N_POS_SPLIT] = 1.0
    return jnp.asarray(feat, BF16), jnp.asarray(coef, F32), jnp.asarray(base, F32)


def _attn_kernel(c_ref, q_ref, k_ref, vt_ref, feat_ref, coef_ref, base_ref,
                 lq1_ref, lk1_ref, lq2_ref, lk2_ref, hn_ref,
                 o_ref, qz_ref, acc1_ref, acc2_ref, s_even_ref, s_odd_ref, *, tq, tk, seq, lam_init):
    head = pl.program_id(0)
    qi = pl.program_id(1)
    c = c_ref[head]
    q0 = qi * tq

    q = q_ref[...]
    lane = lax.broadcasted_iota(jnp.int32, q.shape, 1)
    zero = jnp.zeros_like(q)
    q1z = jnp.where(lane < DIFF_HEAD_DIM, q, zero)
    q2z = jnp.where(lane >= DIFF_HEAD_DIM, q, zero)
    pos_after = (base_ref[...] * coef_ref[...]).astype(BF16)
    for side, pos in ((0, -pos_after), (1, pos_after)):
        qz_ref[side, 0:tq, 0:V_HEAD_DIM] = q1z
        qz_ref[side, tq:2 * tq, 0:V_HEAD_DIM] = q2z
        qz_ref[side, 0:tq, V_HEAD_DIM:] = pos
        qz_ref[side, tq:2 * tq, V_HEAD_DIM:] = pos

    acc1_ref[...] = jnp.zeros_like(acc1_ref)
    acc2_ref[...] = jnp.zeros_like(acc2_ref)

    def update(s, off, m, l, acc_ref, vtc):
        m_new = jnp.maximum(m, jnp.max(s, axis=0, keepdims=True) - off)
        alpha = jnp.exp2(m - m_new)
        p = jnp.exp2(s - (m_new + off))
        l_new = alpha * l + jnp.sum(p, axis=0, keepdims=True)
        acc_ref[...] = alpha * acc_ref[...] + _dot(vtc, p.astype(BF16))
        return m_new, l_new

    def load_chunk(j):
        k0 = pl.multiple_of(j * tk, tk)
        return k_ref[pl.ds(k0, tk), :], vt_ref[:, pl.ds(k0, tk)]

    n_pairs = seq // (2 * tk)
    diag_pair = lax.div(q0, 2 * tk)

    carry = (jnp.full((1, tq), NEG_BIG, F32), jnp.zeros((1, tq), F32),
             jnp.full((1, tq), NEG_BIG, F32), jnp.zeros((1, tq), F32))
    qzd = jnp.concatenate([q1z, q2z], axis=0)
    rel = (lax.broadcasted_iota(jnp.int32, (tk, tq), 0)
           - lax.broadcasted_iota(jnp.int32, (tk, tq), 1)).astype(F32)
    for half in range(2):
        j = 2 * diag_pair + half
        kc, vtc = load_chunk(j)
        s = _dot_nt(kc, qzd)
        bias = c * jnp.abs(rel + (j * tk - q0).astype(F32))
        m1, l1, m2, l2 = carry
        m1, l1 = update(s[:, :tq] - bias, 0.0, m1, l1, acc1_ref, vtc)
        m2, l2 = update(s[:, tq:] - bias, 0.0, m2, l2, acc2_ref, vtc)
        carry = (m1, l1, m2, l2)

    n_off = 2 * (n_pairs - 1)
    first_after = 2 * diag_pair

    def chunk_of(n):
        after = (n >= first_after).astype(jnp.int32)
        return n + 2 * after, after

    def scores(n):
        j, after = chunk_of(n)
        kc = k_ref[pl.ds(pl.multiple_of(j * tk, tk), tk), :]
        return _dot_nt(jnp.concatenate([kc, feat_ref[...]], axis=1), qz_ref[after])

    def consume(s_ref, n, carry):
        j, _ = chunk_of(n)
        vtc = vt_ref[:, pl.ds(pl.multiple_of(j * tk, tk), tk)]
        off = c * jnp.abs(j * tk - q0).astype(F32)
        m1, l1, m2, l2 = carry
        m1, l1 = update(s_ref[:, :tq], off, m1, l1, acc1_ref, vtc)
        m2, l2 = update(s_ref[:, tq:], off, m2, l2, acc2_ref, vtc)
        return m1, l1, m2, l2

    s_even_ref[...] = scores(0)

    def body(t, carry):
        a = 2 * t
        s_odd_ref[...] = scores(a + 1)
        carry = consume(s_even_ref, a, carry)
        s_even_ref[...] = scores(jnp.minimum(a + 2, n_off - 1))
        return consume(s_odd_ref, a + 1, carry)

    m1, l1, m2, l2 = lax.fori_loop(0, n_off // 2, body, carry)

    lam = (jnp.exp(jnp.sum(lq1_ref[...] * lk1_ref[...], axis=-1, keepdims=True))
           - jnp.exp(jnp.sum(lq2_ref[...] * lk2_ref[...], axis=-1, keepdims=True))
           + lam_init)
    o_t = acc1_ref[...] / l1 - lam * (acc2_ref[...] / l2)
    o = o_t.T
    o_ref[...] = (_rmsnorm_rows(o, hn_ref[...]) * (1.0 - lam_init)).astype(o_ref.dtype)


def _attention(q, k, vt, lq1, lk1, lq2, lk2, hn, *, tq, tk, lam_init):
    s = q.shape[0]
    c = jnp.asarray([2.0 ** (-8.0 * (h + 1) / N_ATTN_HEADS) * LOG2E for h in range(N_ATTN_HEADS)], F32)
    feat, coef, base = _alibi_tables(tq, tk)
    lam_spec = pl.BlockSpec((1, DIFF_HEAD_DIM), lambda h, i: (0, 0))
    kern = functools.partial(_attn_kernel, tq=tq, tk=tk, seq=s, lam_init=lam_init)
    return pl.pallas_call(
        kern,
        grid=(N_ATTN_HEADS, s // tq),
        in_specs=[
            pl.BlockSpec(memory_space=pltpu.SMEM),
            pl.BlockSpec((tq, V_HEAD_DIM), lambda h, i: (i, h)),
            pl.BlockSpec((s, V_HEAD_DIM), lambda h, i: (0, h)),
            pl.BlockSpec((V_HEAD_DIM, s), lambda h, i: (h, 0)),
            pl.BlockSpec((None, tk, V_HEAD_DIM), lambda h, i: (h, 0, 0)),
            pl.BlockSpec((None, 1, V_HEAD_DIM), lambda h, i: (h, 0, 0)),
            pl.BlockSpec((tq, V_HEAD_DIM), lambda h, i: (0, 0)),
            lam_spec, lam_spec, lam_spec, lam_spec,
            pl.BlockSpec((1, V_HEAD_DIM), lambda h, i: (0, 0)),
        ],
        out_specs=pl.BlockSpec((tq, V_HEAD_DIM), lambda h, i: (i, h)),
        out_shape=jax.ShapeDtypeStruct((s, ATTN_WIDTH), BF16),
        scratch_shapes=[pltpu.VMEM((2, 2 * tq, 2 * V_HEAD_DIM), BF16),
                        pltpu.VMEM((V_HEAD_DIM, tq), F32), pltpu.VMEM((V_HEAD_DIM, tq), F32),
                        pltpu.VMEM((tk, 2 * tq), F32), pltpu.VMEM((tk, 2 * tq), F32)],
        compiler_params=pltpu.CompilerParams(
            dimension_semantics=("arbitrary", "arbitrary"), vmem_limit_bytes=VMEM_LIMIT_BYTES),
        name="diff_attention",
    )(c, q, k, vt, feat, coef, base, lq1, lk1, lq2, lk2, hn)


def _conv_kernel(prev_ref, main_ref, next_ref, w_ref, b_ref, g_ref, beta_ref, o_ref, buf_ref,
                 *, ts, rows):
    i = pl.program_id(0)
    last = pl.num_programs(0) - 1
    buf_ref[0:CONV_HALO, :] = jnp.where(i > 0, prev_ref[...], 0.0)
    buf_ref[CONV_HALO:CONV_HALO + ts, :] = main_ref[...]
    buf_ref[CONV_HALO + ts:2 * CONV_HALO + ts, :] = jnp.where(i < last, next_ref[...], 0.0)
    first_tap = CONV_HALO - CONV_KERNEL // 2
    for r0 in range(0, ts, rows):
        acc = jnp.broadcast_to(b_ref[...], (rows, CONV_WIDTH))
        for t in range(CONV_KERNEL):
            start = r0 + first_tap + t
            acc = acc + w_ref[t:t + 1, :] * buf_ref[start:start + rows, :]
        mu = jnp.mean(acc, axis=-1, keepdims=True)
        xc = acc - mu
        y = xc * lax.rsqrt(jnp.mean(xc * xc, axis=-1, keepdims=True) + EPS)
        y = y * g_ref[...] + beta_ref[...]
        o_ref[r0:r0 + rows, :] = (y * jax.nn.sigmoid(y)).astype(o_ref.dtype)


def _conformer_conv(u, w, b, g, beta, *, ts, rows):
    s = u.shape[0]
    nh = ts // CONV_HALO
    n_halo_blocks = s // CONV_HALO
    const = lambda i: (0, 0)
    kern = functools.partial(_conv_kernel, ts=ts, rows=rows)
    return pl.pallas_call(
        kern,
        grid=(s // ts,),
        in_specs=[
            pl.BlockSpec((CONV_HALO, CONV_WIDTH), lambda i: (jnp.maximum(i * nh - 1, 0), 0)),
            pl.BlockSpec((ts, CONV_WIDTH), lambda i: (i, 0)),
            pl.BlockSpec((CONV_HALO, CONV_WIDTH),
                         lambda i: (jnp.minimum((i + 1) * nh, n_halo_blocks - 1), 0)),
            pl.BlockSpec((CONV_KERNEL, CONV_WIDTH), const),
            pl.BlockSpec((1, CONV_WIDTH), const),
            pl.BlockSpec((1, CONV_WIDTH), const),
            pl.BlockSpec((1, CONV_WIDTH), const),
        ],
        out_specs=pl.BlockSpec((ts, CONV_WIDTH), lambda i: (i, 0)),
        out_shape=jax.ShapeDtypeStruct((s, CONV_WIDTH), BF16),
        scratch_shapes=[pltpu.VMEM((ts + 2 * CONV_HALO, CONV_WIDTH), F32)],
        compiler_params=pltpu.CompilerParams(
            dimension_semantics=("arbitrary",), vmem_limit_bytes=VMEM_LIMIT_BYTES),
        name="conformer_conv",
    )(u, u, u, w, b, g, beta)


FFN_CHUNKS = ((0, 1024), (1024, 2048), (2048, FFN_HIDDEN))


def _tail_kernel(x_ref, a_ref, c_ref, p_ref, woa_ref, woc_ref, fg_ref, wg_ref, wu_ref, wd_ref,
                 pg_ref, wpg_ref, wpp_ref, o_ref, act_ref):
    x1 = x_ref[...] + _dot(a_ref[...], woa_ref[...]) + _dot(c_ref[...], woc_ref[...])
    h = _rmsnorm_rows(x1, fg_ref[...]).astype(BF16)
    for c0, c1 in FFN_CHUNKS:
        g = _dot(h, wg_ref[:, c0:c1])
        u = _dot(h, wu_ref[:, c0:c1])
        act_ref[:, c0:c1] = (g * jax.nn.sigmoid(g) * u).astype(BF16)
    x2 = x1 + _dot(act_ref[...], wd_ref[...])
    hp = _rmsnorm_rows(x2, pg_ref[...]).astype(BF16)
    gate = jax.nn.sigmoid(_dot(hp, wpg_ref[...]))
    o_ref[...] = x2 + gate * _dot(p_ref[...].astype(BF16), wpp_ref[...])


def _tail(x, a, c, p, woa, woc, fg, wg, wu, wd, pg, wpg, wpp, *, tm):
    s = x.shape[0]
    const = lambda i: (0, 0)

    def resident(shape):
        return pl.BlockSpec(shape, const, pipeline_mode=pl.Buffered(1))

    return pl.pallas_call(
        _tail_kernel,
        grid=(s // tm,),
        in_specs=[
            pl.BlockSpec((tm, D_MODEL), lambda i: (i, 0)),
            pl.BlockSpec((tm, ATTN_WIDTH), lambda i: (i, 0)),
            pl.BlockSpec((tm, CONV_WIDTH), lambda i: (i, 0)),
            pl.BlockSpec((tm, PLE_DIM), lambda i: (i, 0)),
            resident((ATTN_WIDTH, D_MODEL)),
            resident((CONV_WIDTH, D_MODEL)),
            resident((1, D_MODEL)),
            resident((D_MODEL, FFN_HIDDEN)),
            resident((D_MODEL, FFN_HIDDEN)),
            resident((FFN_HIDDEN, D_MODEL)),
            resident((1, D_MODEL)),
            resident((D_MODEL, D_MODEL)),
            resident((PLE_DIM, D_MODEL)),
        ],
        out_specs=pl.BlockSpec((tm, D_MODEL), lambda i: (i, 0)),
        out_shape=jax.ShapeDtypeStruct((s, D_MODEL), F32),
        scratch_shapes=[pltpu.VMEM((tm, FFN_HIDDEN), BF16)],
        compiler_params=pltpu.CompilerParams(
            dimension_semantics=("arbitrary",), vmem_limit_bytes=VMEM_LIMIT_BYTES),
        name="tail",
    )(x, a, c, p, woa, woc, fg, wg, wu, wd, pg, wpg, wpp)


def _layer(x, p, lam_init, attn_norm, w_in, q_norm, k_norm, lq1, lk1, lq2, lk2, head_norm,
           conv_w, conv_b, conv_ln_g, conv_ln_b, w_out, ffn_norm, w_gate, w_up, w_down,
           ple_norm, w_ple_gate, w_ple_proj):
    row = lambda v: v.reshape(1, -1).astype(F32)
    a = ATTN_WIDTH
    wqk = w_in[:, :2 * a].astype(BF16)
    wvt = w_in[:, 2 * a:3 * a].T.astype(BF16)
    wc = w_in[:, 3 * a:].astype(BF16)
    n_seg = ATTN_WIDTH // DIFF_HEAD_DIM
    seg = jnp.asarray(np.kron(np.eye(n_seg), np.full((DIFF_HEAD_DIM, DIFF_HEAD_DIM),
                                                      1.0 / DIFF_HEAD_DIM)), BF16)
    qn = row(jnp.tile(q_norm, n_seg))
    kn = row(jnp.tile(k_norm, n_seg))
    q, k, vt, u = _inproj(x, row(attn_norm), wqk, wvt, wc, qn, kn, seg, tm=512)

    attn = _attention(q, k, vt, row(lq1), row(lk1), row(lq2), row(lk2), row(head_norm),
                      tq=256, tk=512, lam_init=lam_init)
    conv = _conformer_conv(u, conv_w.astype(F32), row(conv_b), row(conv_ln_g), row(conv_ln_b),
                           ts=512, rows=64)
    return _tail(x, attn, conv, p, w_out[:a].astype(BF16), w_out[a:].astype(BF16), row(ffn_norm),
                 w_gate.astype(BF16), w_up.astype(BF16), w_down.astype(BF16), row(ple_norm),
                 w_ple_gate.astype(BF16), w_ple_proj.astype(BF16), tm=512)


def kernel(x, p, attn_norm, w_in, q_norm, k_norm, lambda_q1, lambda_k1, lambda_q2, lambda_k2,
           head_norm, conv_w, conv_b, conv_ln_g, conv_ln_b, w_out, ffn_norm, w_gate, w_up, w_down,
           ple_norm, w_ple_gate, w_ple_proj):
    depth = w_in.shape[0]
    batch = x.shape[0]
    outs = []
    for b in range(batch):
        xb = x[b]
        for i in range(depth):
            lam_init = 0.8 - 0.6 * math.exp(-0.3 * i)
            xb = _layer(xb, p[i, b], lam_init, attn_norm[i], w_in[i], q_norm[i], k_norm[i],
                        lambda_q1[i], lambda_k1[i], lambda_q2[i], lambda_k2[i], head_norm[i],
                        conv_w[i], conv_b[i], conv_ln_g[i], conv_ln_b[i], w_out[i], ffn_norm[i],
                        w_gate[i], w_up[i], w_down[i], ple_norm[i], w_ple_gate[i], w_ple_proj[i])
        outs.append(xb)
    return jnp.stack(outs, axis=0)
```

```python
import functools
import math

import numpy as np
import jax
import jax.numpy as jnp
from jax import lax
from jax.experimental import pallas as pl
from jax.experimental.pallas import tpu as pltpu

D_MODEL = 1024
ATTN_WIDTH = 512
CONV_WIDTH = 512
DIFF_HEAD_DIM = 64
V_HEAD_DIM = 128
V_ROWS = V_HEAD_DIM + 16
N_ATTN_HEADS = 4
CONV_KERNEL = 31
CONV_HALO = 16
FFN_HIDDEN = 2816
PLE_DIM = 256
EPS = 1e-6

VMEM_LIMIT_BYTES = 56 * 1024 * 1024

F32 = jnp.float32
BF16 = jnp.bfloat16
NEG_BIG = -1e30
LOG2E = math.log2(math.e)


def _rmsnorm_rows(x, gain):
    ms = jnp.mean(x * x, axis=-1, keepdims=True)
    return x * lax.rsqrt(ms + EPS) * gain


def _dot(a, b):
    return jnp.dot(a, b, preferred_element_type=F32)


def _dot_nt(a, b):
    return lax.dot_general(a, b, (((1,), (1,)), ((), ())), preferred_element_type=F32)


def _inproj_kernel(x_ref, g_ref, wqk_ref, wvt_ref, wc_ref, qn_ref, kn_ref, seg_ref,
                   q_ref, k_ref, vt_ref, u_ref):
    h = _rmsnorm_rows(x_ref[...], g_ref[...]).astype(BF16)
    qk = _dot(h, wqk_ref[...])

    def seg_norm(y, gain):
        sq = y * y
        hi = sq.astype(BF16)
        lo = (sq - hi.astype(F32)).astype(BF16)
        ms = _dot(hi, seg_ref[...]) + _dot(lo, seg_ref[...])
        return y * lax.rsqrt(ms + EPS) * gain

    q = seg_norm(qk[:, :ATTN_WIDTH], qn_ref[...]) * (DIFF_HEAD_DIM ** -0.5 * LOG2E)
    k = seg_norm(qk[:, ATTN_WIDTH:], kn_ref[...])
    q_ref[...] = q.astype(BF16)
    k_ref[...] = k.astype(BF16)
    vt = _dot_nt(wvt_ref[...], h).astype(BF16)
    ones = jnp.ones((V_ROWS - V_HEAD_DIM, vt.shape[1]), BF16)
    for hd in range(N_ATTN_HEADS):
        vt_ref[hd * V_ROWS:hd * V_ROWS + V_HEAD_DIM, :] = vt[hd * V_HEAD_DIM:(hd + 1) * V_HEAD_DIM]
        vt_ref[hd * V_ROWS + V_HEAD_DIM:(hd + 1) * V_ROWS, :] = ones
    c = _dot(h, wc_ref[...])
    u_ref[...] = c[:, :CONV_WIDTH] * jax.nn.sigmoid(c[:, CONV_WIDTH:])


def _inproj(x, gain, wqk, wvt, wc, qn, kn, seg, *, tm):
    s = x.shape[0]
    const = lambda i: (0, 0)
    return pl.pallas_call(
        _inproj_kernel,
        grid=(s // tm,),
        in_specs=[
            pl.BlockSpec((tm, D_MODEL), lambda i: (i, 0)),
            pl.BlockSpec((1, D_MODEL), const),
            pl.BlockSpec((D_MODEL, 2 * ATTN_WIDTH), const),
            pl.BlockSpec((ATTN_WIDTH, D_MODEL), const),
            pl.BlockSpec((D_MODEL, 2 * CONV_WIDTH), const),
            pl.BlockSpec((1, ATTN_WIDTH), const),
            pl.BlockSpec((1, ATTN_WIDTH), const),
            pl.BlockSpec((ATTN_WIDTH, ATTN_WIDTH), const),
        ],
        out_specs=[
            pl.BlockSpec((tm, ATTN_WIDTH), lambda i: (i, 0)),
            pl.BlockSpec((tm, ATTN_WIDTH), lambda i: (i, 0)),
            pl.BlockSpec((N_ATTN_HEADS * V_ROWS, tm), lambda i: (0, i)),
            pl.BlockSpec((tm, CONV_WIDTH), lambda i: (i, 0)),
        ],
        out_shape=[
            jax.ShapeDtypeStruct((s, ATTN_WIDTH), BF16),
            jax.ShapeDtypeStruct((s, ATTN_WIDTH), BF16),
            jax.ShapeDtypeStruct((N_ATTN_HEADS * V_ROWS, s), BF16),
            jax.ShapeDtypeStruct((s, CONV_WIDTH), F32),
        ],
        compiler_params=pltpu.CompilerParams(
            dimension_semantics=("arbitrary",), vmem_limit_bytes=VMEM_LIMIT_BYTES),
        name="inproj",
    )(x, gain, wqk, wvt, wc, qn, kn, seg)


N_POS_SPLIT = 3
POS_HI_STEP = 16


def _bf16_pieces(value, n):
    pieces, rest = [], np.float64(value)
    for _ in range(n):
        piece = np.float64(np.asarray(rest, dtype=BF16).astype(np.float64))
        pieces.append(piece)
        rest = rest - piece
    return pieces


def _alibi_tables(tq, tk):
    assert tk // POS_HI_STEP <= 256 and tq <= 256
    pos = np.arange(tk)
    feat = np.zeros((N_ATTN_HEADS, tk, V_HEAD_DIM), np.float64)
    qpos = np.zeros((N_ATTN_HEADS, V_HEAD_DIM, tq), np.float64)
    n_lin = 2 * N_POS_SPLIT
    for h in range(N_ATTN_HEADS):
        c = 2.0 ** (-8.0 * (h + 1) / N_ATTN_HEADS) * LOG2E
        for s, piece in enumerate(_bf16_pieces(c, N_POS_SPLIT)):
            feat[h, :, 2 * s] = pos // POS_HI_STEP
            feat[h, :, 2 * s + 1] = pos % POS_HI_STEP
            feat[h, :, n_lin + s] = piece
            qpos[h, 2 * s, :] = -POS_HI_STEP * piece
            qpos[h, 2 * s + 1, :] = -piece
        qpos[h, n_lin:n_lin + N_POS_SPLIT, :] = np.arange(tq)[None, :]
    return jnp.asarray(feat, BF16), jnp.asarray(qpos, BF16)


def _attn_kernel(c_ref, q_ref, k_ref, vt_ref, feat_ref, qpos_ref,
                 lq1_ref, lk1_ref, lq2_ref, lk2_ref, hn_ref,
                 o_ref, qz_ref, acc1_ref, acc2_ref, s_even_ref, s_odd_ref,
                 *, tq, tk, seq, lam_init):
    head = pl.program_id(0)
    qi = pl.program_id(1)
    c = c_ref[head]
    q0 = qi * tq

    qt = q_ref[...].astype(F32).T.astype(BF16)
    row = lax.broadcasted_iota(jnp.int32, qt.shape, 0)
    zero = jnp.zeros_like(qt)
    q1t = jnp.where(row < DIFF_HEAD_DIM, qt, zero)
    q2t = jnp.where(row >= DIFF_HEAD_DIM, qt, zero)
    pos_after = qpos_ref[...]
    for side, pos in ((0, -pos_after), (1, pos_after)):
        qz_ref[side, 0:V_HEAD_DIM, 0:tq] = q1t
        qz_ref[side, 0:V_HEAD_DIM, tq:2 * tq] = q2t
        qz_ref[side, V_HEAD_DIM:, 0:tq] = pos
        qz_ref[side, V_HEAD_DIM:, tq:2 * tq] = pos

    acc1_ref[...] = jnp.zeros_like(acc1_ref)
    acc2_ref[...] = jnp.zeros_like(acc2_ref)

    n_off = seq // tk - 1
    diag = lax.div(q0, tk)

    def chunk_of(n):
        after = (n >= diag).astype(jnp.int32)
        return n + after, after

    def key_rows(j):
        return k_ref[pl.ds(pl.multiple_of(j * tk, tk), tk), :]

    def produce(n, s_ref):
        j, after = chunk_of(n)
        s = _dot(jnp.concatenate([key_rows(j), feat_ref[...]], axis=1), qz_ref[after])
        s_ref[...] = s
        return jnp.max(s, axis=0, keepdims=True)

    def produce_diag(s_ref):
        s = _dot(key_rows(diag), jnp.concatenate([q1t, q2t], axis=1))
        rel = (lax.broadcasted_iota(jnp.int32, (tk, tq), 0)
               - lax.broadcasted_iota(jnp.int32, (tk, tq), 1)).astype(F32)
        bias = c * jnp.abs(rel + (diag * tk - q0).astype(F32))
        s = s - jnp.concatenate([bias, bias], axis=1)
        s_ref[...] = s
        return jnp.max(s, axis=0, keepdims=True)

    def consume(s_ref, cmax, j, off, m):
        vtc = vt_ref[:, pl.ds(pl.multiple_of(j * tk, tk), tk)]
        m_new = jnp.maximum(m, cmax - off)
        alpha = jnp.exp2(m - m_new)
        shift = m_new + off
        for acc_ref, cols in ((acc1_ref, slice(0, tq)), (acc2_ref, slice(tq, 2 * tq))):
            p = jnp.exp2((s_ref[:, cols] - shift[:, cols]).astype(BF16))
            acc_ref[...] = alpha[:, cols] * acc_ref[...] + _dot(vtc, p)
        return m_new

    def consume_off(s_ref, cmax, n, m):
        j, _ = chunk_of(n)
        return consume(s_ref, cmax, j, c * jnp.abs(j * tk - q0).astype(F32), m)

    m = jnp.full((1, 2 * tq), NEG_BIG, F32)
    cm_even = produce_diag(s_even_ref)
    cm_odd = produce(0, s_odd_ref)
    m = consume(s_even_ref, cm_even, diag, 0.0, m)
    cm_even = produce(1, s_even_ref)
    m = consume_off(s_odd_ref, cm_odd, 0, m)

    def body(t, carry):
        m, cm_even = carry
        a = 2 * t + 1
        cm_odd = produce(a + 1, s_odd_ref)
        m = consume_off(s_even_ref, cm_even, a, m)
        cm_even = produce(jnp.minimum(a + 2, n_off - 1), s_even_ref)
        m = consume_off(s_odd_ref, cm_odd, a + 1, m)
        return m, cm_even

    assert n_off % 2 == 1
    lax.fori_loop(0, n_off // 2, body, (m, cm_even))

    lam = (jnp.exp(jnp.sum(lq1_ref[...] * lk1_ref[...], axis=-1, keepdims=True))
           - jnp.exp(jnp.sum(lq2_ref[...] * lk2_ref[...], axis=-1, keepdims=True))
           + lam_init)
    o_t = (acc1_ref[0:V_HEAD_DIM, :] / acc1_ref[V_HEAD_DIM:V_HEAD_DIM + 1, :]
           - lam * (acc2_ref[0:V_HEAD_DIM, :] / acc2_ref[V_HEAD_DIM:V_HEAD_DIM + 1, :]))
    o = o_t.T
    o_ref[...] = (_rmsnorm_rows(o, hn_ref[...]) * (1.0 - lam_init)).astype(o_ref.dtype)


def _attention(q, k, vt, lq1, lk1, lq2, lk2, hn, *, tq, tk, lam_init):
    s = q.shape[0]
    c = jnp.asarray([2.0 ** (-8.0 * (h + 1) / N_ATTN_HEADS) * LOG2E for h in range(N_ATTN_HEADS)], F32)
    feat, qpos = _alibi_tables(tq, tk)
    lam_spec = pl.BlockSpec((1, DIFF_HEAD_DIM), lambda h, i: (0, 0))
    kern = functools.partial(_attn_kernel, tq=tq, tk=tk, seq=s, lam_init=lam_init)
    return pl.pallas_call(
        kern,
        grid=(N_ATTN_HEADS, s // tq),
        in_specs=[
            pl.BlockSpec(memory_space=pltpu.SMEM),
            pl.BlockSpec((tq, V_HEAD_DIM), lambda h, i: (i, h)),
            pl.BlockSpec((s, V_HEAD_DIM), lambda h, i: (0, h)),
            pl.BlockSpec((V_ROWS, s), lambda h, i: (h, 0)),
            pl.BlockSpec((None, tk, V_HEAD_DIM), lambda h, i: (h, 0, 0)),
            pl.BlockSpec((None, V_HEAD_DIM, tq), lambda h, i: (h, 0, 0)),
            lam_spec, lam_spec, lam_spec, lam_spec,
            pl.BlockSpec((1, V_HEAD_DIM), lambda h, i: (0, 0)),
        ],
        out_specs=pl.BlockSpec((tq, V_HEAD_DIM), lambda h, i: (i, h)),
        out_shape=jax.ShapeDtypeStruct((s, ATTN_WIDTH), BF16),
        scratch_shapes=[pltpu.VMEM((2, 2 * V_HEAD_DIM, 2 * tq), BF16),
                        pltpu.VMEM((V_ROWS, tq), F32), pltpu.VMEM((V_ROWS, tq), F32),
                        pltpu.VMEM((tk, 2 * tq), F32), pltpu.VMEM((tk, 2 * tq), F32)],
        compiler_params=pltpu.CompilerParams(
            dimension_semantics=("arbitrary", "arbitrary"), vmem_limit_bytes=VMEM_LIMIT_BYTES),
        name="diff_attention",
    )(c, q, k, vt, feat, qpos, lq1, lk1, lq2, lk2, hn)


def _conv_kernel(prev_ref, main_ref, next_ref, w_ref, b_ref, g_ref, beta_ref, o_ref, buf_ref,
                 *, ts, rows):
    i = pl.program_id(0)
    last = pl.num_programs(0) - 1
    buf_ref[0:CONV_HALO, :] = jnp.where(i > 0, prev_ref[...], 0.0)
    buf_ref[CONV_HALO:CONV_HALO + ts, :] = main_ref[...]
    buf_ref[CONV_HALO + ts:2 * CONV_HALO + ts, :] = jnp.where(i < last, next_ref[...], 0.0)
    first_tap = CONV_HALO - CONV_KERNEL // 2
    for r0 in range(0, ts, rows):
        acc = jnp.broadcast_to(b_ref[...], (rows, CONV_WIDTH))
        for t in range(CONV_KERNEL):
            start = r0 + first_tap + t
            acc = acc + w_ref[t:t + 1, :] * buf_ref[start:start + rows, :]
        mu = jnp.mean(acc, axis=-1, keepdims=True)
        xc = acc - mu
        y = xc * lax.rsqrt(jnp.mean(xc * xc, axis=-1, keepdims=True) + EPS)
        y = y * g_ref[...] + beta_ref[...]
        o_ref[r0:r0 + rows, :] = (y * jax.nn.sigmoid(y)).astype(o_ref.dtype)


def _conformer_conv(u, w, b, g, beta, *, ts, rows):
    s = u.shape[0]
    nh = ts // CONV_HALO
    n_halo_blocks = s // CONV_HALO
    const = lambda i: (0, 0)
    kern = functools.partial(_conv_kernel, ts=ts, rows=rows)
    return pl.pallas_call(
        kern,
        grid=(s // ts,),
        in_specs=[
            pl.BlockSpec((CONV_HALO, CONV_WIDTH), lambda i: (jnp.maximum(i * nh - 1, 0), 0)),
            pl.BlockSpec((ts, CONV_WIDTH), lambda i: (i, 0)),
            pl.BlockSpec((CONV_HALO, CONV_WIDTH),
                         lambda i: (jnp.minimum((i + 1) * nh, n_halo_blocks - 1), 0)),
            pl.BlockSpec((CONV_KERNEL, CONV_WIDTH), const),
            pl.BlockSpec((1, CONV_WIDTH), const),
            pl.BlockSpec((1, CONV_WIDTH), const),
            pl.BlockSpec((1, CONV_WIDTH), const),
        ],
        out_specs=pl.BlockSpec((ts, CONV_WIDTH), lambda i: (i, 0)),
        out_shape=jax.ShapeDtypeStruct((s, CONV_WIDTH), BF16),
        scratch_shapes=[pltpu.VMEM((ts + 2 * CONV_HALO, CONV_WIDTH), F32)],
        compiler_params=pltpu.CompilerParams(
            dimension_semantics=("arbitrary",), vmem_limit_bytes=VMEM_LIMIT_BYTES),
        name="conformer_conv",
    )(u, u, u, w, b, g, beta)


FFN_CHUNKS = ((0, 1024), (1024, 2048), (2048, FFN_HIDDEN))


def _tail_kernel(x_ref, a_ref, c_ref, p_ref, woa_ref, woc_ref, fg_ref, wg_ref, wu_ref, wd_ref,
                 pg_ref, wpg_ref, wpp_ref, o_ref, act_ref):
    x1 = x_ref[...] + _dot(a_ref[...], woa_ref[...]) + _dot(c_ref[...], woc_ref[...])
    h = _rmsnorm_rows(x1, fg_ref[...]).astype(BF16)
    for c0, c1 in FFN_CHUNKS:
        g = _dot(h, wg_ref[:, c0:c1])
        u = _dot(h, wu_ref[:, c0:c1])
        act_ref[:, c0:c1] = (g * jax.nn.sigmoid(g) * u).astype(BF16)
    x2 = x1 + _dot(act_ref[...], wd_ref[...])
    hp = _rmsnorm_rows(x2, pg_ref[...]).astype(BF16)
    gate = jax.nn.sigmoid(_dot(hp, wpg_ref[...]))
    o_ref[...] = x2 + gate * _dot(p_ref[...].astype(BF16), wpp_ref[...])


def _tail(x, a, c, p, woa, woc, fg, wg, wu, wd, pg, wpg, wpp, *, tm):
    s = x.shape[0]
    const = lambda i: (0, 0)

    def resident(shape):
        return pl.BlockSpec(shape, const, pipeline_mode=pl.Buffered(1))

    return pl.pallas_call(
        _tail_kernel,
        grid=(s // tm,),
        in_specs=[
            pl.BlockSpec((tm, D_MODEL), lambda i: (i, 0)),
            pl.BlockSpec((tm, ATTN_WIDTH), lambda i: (i, 0)),
            pl.BlockSpec((tm, CONV_WIDTH), lambda i: (i, 0)),
            pl.BlockSpec((tm, PLE_DIM), lambda i: (i, 0)),
            resident((ATTN_WIDTH, D_MODEL)),
            resident((CONV_WIDTH, D_MODEL)),
            resident((1, D_MODEL)),
            resident((D_MODEL, FFN_HIDDEN)),
            resident((D_MODEL, FFN_HIDDEN)),
            resident((FFN_HIDDEN, D_MODEL)),
            resident((1, D_MODEL)),
            resident((D_MODEL, D_MODEL)),
            resident((PLE_DIM, D_MODEL)),
        ],
        out_specs=pl.BlockSpec((tm, D_MODEL), lambda i: (i, 0)),
        out_shape=jax.ShapeDtypeStruct((s, D_MODEL), F32),
        scratch_shapes=[pltpu.VMEM((tm, FFN_HIDDEN), BF16)],
        compiler_params=pltpu.CompilerParams(
            dimension_semantics=("arbitrary",), vmem_limit_bytes=VMEM_LIMIT_BYTES),
        name="tail",
    )(x, a, c, p, woa, woc, fg, wg, wu, wd, pg, wpg, wpp)


def _layer(x, p, lam_init, attn_norm, w_in, q_norm, k_norm, lq1, lk1, lq2, lk2, head_norm,
           conv_w, conv_b, conv_ln_g, conv_ln_b, w_out, ffn_norm, w_gate, w_up, w_down,
           ple_norm, w_ple_gate, w_ple_proj):
    row = lambda v: v.reshape(1, -1).astype(F32)
    a = ATTN_WIDTH
    wqk = w_in[:, :2 * a].astype(BF16)
    wvt = w_in[:, 2 * a:3 * a].T.astype(BF16)
    wc = w_in[:, 3 * a:].astype(BF16)
    n_seg = ATTN_WIDTH // DIFF_HEAD_DIM
    seg = jnp.asarray(np.kron(np.eye(n_seg), np.full((DIFF_HEAD_DIM, DIFF_HEAD_DIM),
                                                      1.0 / DIFF_HEAD_DIM)), BF16)
    qn = row(jnp.tile(q_norm, n_seg))
    kn = row(jnp.tile(k_norm, n_seg))
    q, k, vt, u = _inproj(x, row(attn_norm), wqk, wvt, wc, qn, kn, seg, tm=512)

    attn = _attention(q, k, vt, row(lq1), row(lk1), row(lq2), row(lk2), row(head_norm),
                      tq=256, tk=1024, lam_init=lam_init)
    conv = _conformer_conv(u, conv_w.astype(F32), row(conv_b), row(conv_ln_g), row(conv_ln_b),
                           ts=512, rows=64)
    return _tail(x, attn, conv, p, w_out[:a].astype(BF16), w_out[a:].astype(BF16), row(ffn_norm),
                 w_gate.astype(BF16), w_up.astype(BF16), w_down.astype(BF16), row(ple_norm),
                 w_ple_gate.astype(BF16), w_ple_proj.astype(BF16), tm=512)


def kernel(x, p, attn_norm, w_in, q_norm, k_norm, lambda_q1, lambda_k1, lambda_q2, lambda_k2,
           head_norm, conv_w, conv_b, conv_ln_g, conv_ln_b, w_out, ffn_norm, w_gate, w_up, w_down,
           ple_norm, w_ple_gate, w_ple_proj):
    depth = w_in.shape[0]
    batch = x.shape[0]
    outs = []
    for b in range(batch):
        xb = x[b]
        for i in range(depth):
            lam_init = 0.8 - 0.6 * math.exp(-0.3 * i)
            xb = _layer(xb, p[i, b], lam_init, attn_norm[i], w_in[i], q_norm[i], k_norm[i],
                        lambda_q1[i], lambda_k1[i], lambda_q2[i], lambda_k2[i], head_norm[i],
                        conv_w[i], conv_b[i], conv_ln_g[i], conv_ln_b[i], w_out[i], ffn_norm[i],
                        w_gate[i], w_up[i], w_down[i], ple_norm[i], w_ple_gate[i], w_ple_proj[i])
        outs.append(xb)
    return jnp.stack(outs, axis=0)
```

```python
import functools
import math

import numpy as np
import jax
import jax.numpy as jnp
from jax import lax
from jax.experimental import pallas as pl
from jax.experimental.pallas import tpu as pltpu

D_MODEL = 1024
ATTN_WIDTH = 512
CONV_WIDTH = 512
DIFF_HEAD_DIM = 64
V_HEAD_DIM = 128
V_ROWS = V_HEAD_DIM + 16
N_ATTN_HEADS = 4
CONV_KERNEL = 31
CONV_HALO = 16
SUBLANES = 8
FFN_HIDDEN = 2816
PLE_DIM = 256
EPS = 1e-6

VMEM_LIMIT_BYTES = 56 * 1024 * 1024

F32 = jnp.float32
BF16 = jnp.bfloat16
NEG_BIG = -1e30
LOG2E = math.log2(math.e)


def _rmsnorm_rows(x, gain):
    ms = jnp.mean(x * x, axis=-1, keepdims=True)
    return x * lax.rsqrt(ms + EPS) * gain


def _dot(a, b):
    return jnp.dot(a, b, preferred_element_type=F32)


def _dot_nt(a, b):
    return lax.dot_general(a, b, (((1,), (1,)), ((), ())), preferred_element_type=F32)


def _inproj_kernel(x_ref, g_ref, wqk_ref, wvt_ref, wc_ref, qn_ref, kn_ref, seg_ref,
                   q_ref, k_ref, vt_ref, u_ref):
    h = _rmsnorm_rows(x_ref[...], g_ref[...]).astype(BF16)
    qk = _dot(h, wqk_ref[...])

    def seg_norm(y, gain):
        sq = y * y
        hi = sq.astype(BF16)
        lo = (sq - hi.astype(F32)).astype(BF16)
        ms = _dot(hi, seg_ref[...]) + _dot(lo, seg_ref[...])
        return y * lax.rsqrt(ms + EPS) * gain

    q = seg_norm(qk[:, :ATTN_WIDTH], qn_ref[...]) * (DIFF_HEAD_DIM ** -0.5 * LOG2E)
    k = seg_norm(qk[:, ATTN_WIDTH:], kn_ref[...])
    q_ref[...] = q.astype(BF16)
    k_ref[...] = k.astype(BF16)
    vt = _dot_nt(wvt_ref[...], h).astype(BF16)
    ones = jnp.ones((V_ROWS - V_HEAD_DIM, vt.shape[1]), BF16)
    for hd in range(N_ATTN_HEADS):
        vt_ref[hd * V_ROWS:hd * V_ROWS + V_HEAD_DIM, :] = vt[hd * V_HEAD_DIM:(hd + 1) * V_HEAD_DIM]
        vt_ref[hd * V_ROWS + V_HEAD_DIM:(hd + 1) * V_ROWS, :] = ones
    c = _dot(h, wc_ref[...])
    u_ref[...] = c[:, :CONV_WIDTH] * jax.nn.sigmoid(c[:, CONV_WIDTH:])


def _inproj(x, gain, wqk, wvt, wc, qn, kn, seg, *, tm):
    s = x.shape[0]
    const = lambda i: (0, 0)
    return pl.pallas_call(
        _inproj_kernel,
        grid=(s // tm,),
        in_specs=[
            pl.BlockSpec((tm, D_MODEL), lambda i: (i, 0)),
            pl.BlockSpec((1, D_MODEL), const),
            pl.BlockSpec((D_MODEL, 2 * ATTN_WIDTH), const),
            pl.BlockSpec((ATTN_WIDTH, D_MODEL), const),
            pl.BlockSpec((D_MODEL, 2 * CONV_WIDTH), const),
            pl.BlockSpec((1, ATTN_WIDTH), const),
            pl.BlockSpec((1, ATTN_WIDTH), const),
            pl.BlockSpec((ATTN_WIDTH, ATTN_WIDTH), const),
        ],
        out_specs=[
            pl.BlockSpec((tm, ATTN_WIDTH), lambda i: (i, 0)),
            pl.BlockSpec((tm, ATTN_WIDTH), lambda i: (i, 0)),
            pl.BlockSpec((N_ATTN_HEADS * V_ROWS, tm), lambda i: (0, i)),
            pl.BlockSpec((tm, CONV_WIDTH), lambda i: (i, 0)),
        ],
        out_shape=[
            jax.ShapeDtypeStruct((s, ATTN_WIDTH), BF16),
            jax.ShapeDtypeStruct((s, ATTN_WIDTH), BF16),
            jax.ShapeDtypeStruct((N_ATTN_HEADS * V_ROWS, s), BF16),
            jax.ShapeDtypeStruct((s, CONV_WIDTH), F32),
        ],
        compiler_params=pltpu.CompilerParams(
            dimension_semantics=("arbitrary",), vmem_limit_bytes=VMEM_LIMIT_BYTES),
        name="inproj",
    )(x, gain, wqk, wvt, wc, qn, kn, seg)


N_POS_SPLIT = 3
POS_HI_STEP = 16
BF16_EXACT_INT = 256
EXP_STRIP = 64


def _bf16_pieces(value, n):
    pieces, rest = [], np.float64(value)
    for _ in range(n):
        piece = np.float64(np.asarray(rest, dtype=BF16).astype(np.float64))
        pieces.append(piece)
        rest = rest - piece
    return pieces


def _alibi_tables(tq, tk):
    assert tk // POS_HI_STEP <= BF16_EXACT_INT and tq <= 2 * BF16_EXACT_INT
    pos = np.arange(tk)
    qidx = np.arange(tq)
    feat = np.zeros((N_ATTN_HEADS, tk, V_HEAD_DIM), np.float64)
    qpos = np.zeros((N_ATTN_HEADS, V_HEAD_DIM, tq), np.float64)
    n_lin = 2 * N_POS_SPLIT
    for h in range(N_ATTN_HEADS):
        c = 2.0 ** (-8.0 * (h + 1) / N_ATTN_HEADS) * LOG2E
        for s, piece in enumerate(_bf16_pieces(c, N_POS_SPLIT)):
            feat[h, :, 2 * s] = pos // POS_HI_STEP
            feat[h, :, 2 * s + 1] = pos % POS_HI_STEP
            qpos[h, 2 * s, :] = -POS_HI_STEP * piece
            qpos[h, 2 * s + 1, :] = -piece
            feat[h, :, n_lin + 2 * s] = piece
            feat[h, :, n_lin + 2 * s + 1] = piece
            qpos[h, n_lin + 2 * s, :] = qidx % BF16_EXACT_INT
            qpos[h, n_lin + 2 * s + 1, :] = qidx - qidx % BF16_EXACT_INT
    return jnp.asarray(feat, BF16), jnp.asarray(qpos, BF16)


def _attn_kernel(c_ref, q_ref, k_ref, vt_ref, feat_ref, qpos_ref,
                 lq1_ref, lk1_ref, lq2_ref, lk2_ref, hn_ref,
                 o_ref, qz_ref, acc1_ref, acc2_ref, s_even_ref, s_odd_ref, p_even_ref, p_odd_ref,
                 *, tq, tk, seq, lam_init):
    head = pl.program_id(0)
    qi = pl.program_id(1)
    c = c_ref[head]
    q0 = qi * tq

    qt = q_ref[...].astype(F32).T.astype(BF16)
    row = lax.broadcasted_iota(jnp.int32, qt.shape, 0)
    zero = jnp.zeros_like(qt)
    q1t = jnp.where(row < DIFF_HEAD_DIM, qt, zero)
    q2t = jnp.where(row >= DIFF_HEAD_DIM, qt, zero)
    pos_after = qpos_ref[...]
    for side, pos in ((0, -pos_after), (1, pos_after)):
        qz_ref[side, 0:V_HEAD_DIM, 0:tq] = q1t
        qz_ref[side, 0:V_HEAD_DIM, tq:2 * tq] = q2t
        qz_ref[side, V_HEAD_DIM:, 0:tq] = pos
        qz_ref[side, V_HEAD_DIM:, tq:2 * tq] = pos

    acc1_ref[...] = jnp.zeros_like(acc1_ref)
    acc2_ref[...] = jnp.zeros_like(acc2_ref)

    n_chunks = seq // tk
    assert n_chunks % 2 == 0
    diag = lax.div(q0, tk)
    s_bufs = (s_even_ref, s_odd_ref)
    p_bufs = (p_even_ref, p_odd_ref)

    def chunk_of(i):
        after = (i > diag).astype(jnp.int32)
        return i - 1 + after, after

    def key_rows(j):
        return k_ref[pl.ds(pl.multiple_of(j * tk, tk), tk), :]

    def scores(i, s_ref):
        j, after = chunk_of(i)
        s = _dot(jnp.concatenate([key_rows(j), feat_ref[...]], axis=1), qz_ref[after])
        s_ref[...] = s
        return jnp.max(s, axis=0, keepdims=True)

    def scores_diag(s_ref):
        s = _dot(key_rows(diag), jnp.concatenate([q1t, q2t], axis=1))
        rel = (lax.broadcasted_iota(jnp.int32, (tk, tq), 0)
               - lax.broadcasted_iota(jnp.int32, (tk, tq), 1)).astype(F32)
        bias = c * jnp.abs(rel + (diag * tk - q0).astype(F32))
        s = s - jnp.concatenate([bias, bias], axis=1)
        s_ref[...] = s
        return jnp.max(s, axis=0, keepdims=True)

    def chunk_offset(i):
        j, _ = chunk_of(i)
        return c * jnp.abs(j * tk - q0).astype(F32)

    def exps(s_ref, p_ref, cmax, off, m):
        m_new = jnp.maximum(m, cmax - off)
        shift = m_new + off
        for r in range(0, tk, EXP_STRIP):
            p_ref[r:r + EXP_STRIP, :] = jnp.exp2((s_ref[r:r + EXP_STRIP, :] - shift).astype(BF16))
        return m_new, jnp.exp2(m - m_new)

    def accumulate(p_ref, alpha, j):
        vtc = vt_ref[:, pl.ds(pl.multiple_of(j * tk, tk), tk)]
        acc1_ref[...] = alpha[:, :tq] * acc1_ref[...] + _dot(vtc, p_ref[:, :tq])
        acc2_ref[...] = alpha[:, tq:] * acc2_ref[...] + _dot(vtc, p_ref[:, tq:])

    def accumulate_visit(p_ref, alpha, i):
        accumulate(p_ref, alpha, jnp.where(i == 0, diag, chunk_of(i)[0]))

    m = jnp.full((1, 2 * tq), NEG_BIG, F32)
    cmax = scores_diag(s_bufs[0])
    m, alpha = exps(s_bufs[0], p_bufs[0], cmax, 0.0, m)
    cmax = scores(1, s_bufs[1])

    def body(t, carry):
        m, cmax, alpha = carry
        for parity in range(2):
            i = 2 * t + parity
            cmax_next = scores(i + 2, s_bufs[parity])
            m, alpha_next = exps(s_bufs[1 - parity], p_bufs[1 - parity], cmax, chunk_offset(i + 1), m)
            accumulate_visit(p_bufs[parity], alpha, i)
            cmax, alpha = cmax_next, alpha_next
        return m, cmax, alpha

    m, cmax, alpha = lax.fori_loop(0, (n_chunks - 2) // 2, body, (m, cmax, alpha))
    last = n_chunks - 1
    m, alpha_last = exps(s_bufs[1], p_bufs[1], cmax, chunk_offset(last), m)
    accumulate_visit(p_bufs[0], alpha, last - 1)
    accumulate_visit(p_bufs[1], alpha_last, last)

    lam = (jnp.exp(jnp.sum(lq1_ref[...] * lk1_ref[...], axis=-1, keepdims=True))
           - jnp.exp(jnp.sum(lq2_ref[...] * lk2_ref[...], axis=-1, keepdims=True))
           + lam_init)
    o_t = (acc1_ref[0:V_HEAD_DIM, :] / acc1_ref[V_HEAD_DIM:V_HEAD_DIM + 1, :]
           - lam * (acc2_ref[0:V_HEAD_DIM, :] / acc2_ref[V_HEAD_DIM:V_HEAD_DIM + 1, :]))
    o = o_t.T
    o_ref[...] = (_rmsnorm_rows(o, hn_ref[...]) * (1.0 - lam_init)).astype(o_ref.dtype)


def _attention(q, k, vt, lq1, lk1, lq2, lk2, hn, *, tq, tk, lam_init):
    s = q.shape[0]
    c = jnp.asarray([2.0 ** (-8.0 * (h + 1) / N_ATTN_HEADS) * LOG2E for h in range(N_ATTN_HEADS)], F32)
    feat, qpos = _alibi_tables(tq, tk)
    lam_spec = pl.BlockSpec((1, DIFF_HEAD_DIM), lambda h, i: (0, 0))
    kern = functools.partial(_attn_kernel, tq=tq, tk=tk, seq=s, lam_init=lam_init)
    return pl.pallas_call(
        kern,
        grid=(N_ATTN_HEADS, s // tq),
        in_specs=[
            pl.BlockSpec(memory_space=pltpu.SMEM),
            pl.BlockSpec((tq, V_HEAD_DIM), lambda h, i: (i, h)),
            pl.BlockSpec((s, V_HEAD_DIM), lambda h, i: (0, h)),
            pl.BlockSpec((V_ROWS, s), lambda h, i: (h, 0)),
            pl.BlockSpec((None, tk, V_HEAD_DIM), lambda h, i: (h, 0, 0)),
            pl.BlockSpec((None, V_HEAD_DIM, tq), lambda h, i: (h, 0, 0)),
            lam_spec, lam_spec, lam_spec, lam_spec,
            pl.BlockSpec((1, V_HEAD_DIM), lambda h, i: (0, 0)),
        ],
        out_specs=pl.BlockSpec((tq, V_HEAD_DIM), lambda h, i: (i, h)),
        out_shape=jax.ShapeDtypeStruct((s, ATTN_WIDTH), BF16),
        scratch_shapes=[pltpu.VMEM((2, 2 * V_HEAD_DIM, 2 * tq), BF16),
                        pltpu.VMEM((V_ROWS, tq), F32), pltpu.VMEM((V_ROWS, tq), F32),
                        pltpu.VMEM((tk, 2 * tq), F32), pltpu.VMEM((tk, 2 * tq), F32),
                        pltpu.VMEM((tk, 2 * tq), BF16), pltpu.VMEM((tk, 2 * tq), BF16)],
        compiler_params=pltpu.CompilerParams(
            dimension_semantics=("arbitrary", "arbitrary"), vmem_limit_bytes=VMEM_LIMIT_BYTES),
        name="diff_attention",
    )(c, q, k, vt, feat, qpos, lq1, lk1, lq2, lk2, hn)


def _conv_kernel(prev_ref, main_ref, next_ref, w_ref, b_ref, g_ref, beta_ref, o_ref, buf_ref,
                 shift_ref, *, ts, rows):
    i = pl.program_id(0)
    last = pl.num_programs(0) - 1
    buf_ref[0:CONV_HALO, :] = jnp.where(i > 0, prev_ref[...], 0.0)
    buf_ref[CONV_HALO:CONV_HALO + ts, :] = main_ref[...]
    buf_ref[CONV_HALO + ts:2 * CONV_HALO + ts, :] = jnp.where(i < last, next_ref[...], 0.0)
    n_shift = shift_ref.shape[1]
    for phase in range(1, SUBLANES):
        shift_ref[phase - 1] = buf_ref[phase:phase + n_shift, :]
    first_tap = CONV_HALO - CONV_KERNEL // 2
    for r0 in range(0, ts, rows):
        acc = jnp.broadcast_to(b_ref[...], (rows // SUBLANES, SUBLANES, CONV_WIDTH))
        for t in range(CONV_KERNEL):
            phase = (first_tap + t) % SUBLANES
            start = r0 + (first_tap + t) - phase
            src = buf_ref if phase == 0 else shift_ref.at[phase - 1]
            window = src[start:start + rows, :].reshape(rows // SUBLANES, SUBLANES, CONV_WIDTH)
            acc = acc + w_ref[t][None] * window
        acc = acc.reshape(rows, CONV_WIDTH)
        mu = jnp.mean(acc, axis=-1, keepdims=True)
        xc = acc - mu
        y = xc * lax.rsqrt(jnp.mean(xc * xc, axis=-1, keepdims=True) + EPS)
        y = y * g_ref[...] + beta_ref[...]
        o_ref[r0:r0 + rows, :] = (y * jax.nn.sigmoid(y)).astype(o_ref.dtype)


def _conformer_conv(u, w, b, g, beta, *, ts, rows):
    s = u.shape[0]
    nh = ts // CONV_HALO
    n_halo_blocks = s // CONV_HALO
    const = lambda i: (0, 0)
    kern = functools.partial(_conv_kernel, ts=ts, rows=rows)
    return pl.pallas_call(
        kern,
        grid=(s // ts,),
        in_specs=[
            pl.BlockSpec((CONV_HALO, CONV_WIDTH), lambda i: (jnp.maximum(i * nh - 1, 0), 0)),
            pl.BlockSpec((ts, CONV_WIDTH), lambda i: (i, 0)),
            pl.BlockSpec((CONV_HALO, CONV_WIDTH),
                         lambda i: (jnp.minimum((i + 1) * nh, n_halo_blocks - 1), 0)),
            pl.BlockSpec((CONV_KERNEL, SUBLANES, CONV_WIDTH), lambda i: (0, 0, 0)),
            pl.BlockSpec((1, CONV_WIDTH), const),
            pl.BlockSpec((1, CONV_WIDTH), const),
            pl.BlockSpec((1, CONV_WIDTH), const),
        ],
        out_specs=pl.BlockSpec((ts, CONV_WIDTH), lambda i: (i, 0)),
        out_shape=jax.ShapeDtypeStruct((s, CONV_WIDTH), BF16),
        scratch_shapes=[pltpu.VMEM((ts + 2 * CONV_HALO, CONV_WIDTH), F32),
                        pltpu.VMEM((SUBLANES - 1, ts + 2 * CONV_HALO - SUBLANES, CONV_WIDTH), F32)],
        compiler_params=pltpu.CompilerParams(
            dimension_semantics=("arbitrary",), vmem_limit_bytes=VMEM_LIMIT_BYTES),
        name="conformer_conv",
    )(u, u, u, w, b, g, beta)


FFN_CHUNKS = ((0, 1024), (1024, 2048), (2048, FFN_HIDDEN))


def _tail_kernel(x_ref, a_ref, c_ref, p_ref, woa_ref, woc_ref, fg_ref, wg_ref, wu_ref, wd_ref,
                 pg_ref, wpg_ref, wpp_ref, o_ref, act_ref):
    x1 = x_ref[...] + _dot(a_ref[...], woa_ref[...]) + _dot(c_ref[...], woc_ref[...])
    h = _rmsnorm_rows(x1, fg_ref[...]).astype(BF16)
    for c0, c1 in FFN_CHUNKS:
        g = _dot(h, wg_ref[:, c0:c1])
        u = _dot(h, wu_ref[:, c0:c1])
        act_ref[:, c0:c1] = (g * jax.nn.sigmoid(g) * u).astype(BF16)
    x2 = x1 + _dot(act_ref[...], wd_ref[...])
    hp = _rmsnorm_rows(x2, pg_ref[...]).astype(BF16)
    gate = jax.nn.sigmoid(_dot(hp, wpg_ref[...]))
    o_ref[...] = x2 + gate * _dot(p_ref[...].astype(BF16), wpp_ref[...])


def _tail(x, a, c, p, woa, woc, fg, wg, wu, wd, pg, wpg, wpp, *, tm):
    s = x.shape[0]
    const = lambda i: (0, 0)

    def resident(shape):
        return pl.BlockSpec(shape, const, pipeline_mode=pl.Buffered(1))

    return pl.pallas_call(
        _tail_kernel,
        grid=(s // tm,),
        in_specs=[
            pl.BlockSpec((tm, D_MODEL), lambda i: (i, 0)),
            pl.BlockSpec((tm, ATTN_WIDTH), lambda i: (i, 0)),
            pl.BlockSpec((tm, CONV_WIDTH), lambda i: (i, 0)),
            pl.BlockSpec((tm, PLE_DIM), lambda i: (i, 0)),
            resident((ATTN_WIDTH, D_MODEL)),
            resident((CONV_WIDTH, D_MODEL)),
            resident((1, D_MODEL)),
            resident((D_MODEL, FFN_HIDDEN)),
            resident((D_MODEL, FFN_HIDDEN)),
            resident((FFN_HIDDEN, D_MODEL)),
            resident((1, D_MODEL)),
            resident((D_MODEL, D_MODEL)),
            resident((PLE_DIM, D_MODEL)),
        ],
        out_specs=pl.BlockSpec((tm, D_MODEL), lambda i: (i, 0)),
        out_shape=jax.ShapeDtypeStruct((s, D_MODEL), F32),
        scratch_shapes=[pltpu.VMEM((tm, FFN_HIDDEN), BF16)],
        compiler_params=pltpu.CompilerParams(
            dimension_semantics=("arbitrary",), vmem_limit_bytes=VMEM_LIMIT_BYTES),
        name="tail",
    )(x, a, c, p, woa, woc, fg, wg, wu, wd, pg, wpg, wpp)


def _layer(x, p, lam_init, attn_norm, w_in, q_norm, k_norm, lq1, lk1, lq2, lk2, head_norm,
           conv_w, conv_b, conv_ln_g, conv_ln_b, w_out, ffn_norm, w_gate, w_up, w_down,
           ple_norm, w_ple_gate, w_ple_proj):
    row = lambda v: v.reshape(1, -1).astype(F32)
    a = ATTN_WIDTH
    wqk = w_in[:, :2 * a].astype(BF16)
    wvt = w_in[:, 2 * a:3 * a].T.astype(BF16)
    wc = w_in[:, 3 * a:].astype(BF16)
    n_seg = ATTN_WIDTH // DIFF_HEAD_DIM
    seg = jnp.asarray(np.kron(np.eye(n_seg), np.full((DIFF_HEAD_DIM, DIFF_HEAD_DIM),
                                                      1.0 / DIFF_HEAD_DIM)), BF16)
    qn = row(jnp.tile(q_norm, n_seg))
    kn = row(jnp.tile(k_norm, n_seg))
    q, k, vt, u = _inproj(x, row(attn_norm), wqk, wvt, wc, qn, kn, seg, tm=512)

    attn = _attention(q, k, vt, row(lq1), row(lk1), row(lq2), row(lk2), row(head_norm),
                      tq=512, tk=1024, lam_init=lam_init)
    conv_taps = jnp.broadcast_to(conv_w.astype(F32)[:, None, :], (CONV_KERNEL, SUBLANES, CONV_WIDTH))
    conv = _conformer_conv(u, conv_taps, row(conv_b), row(conv_ln_g), row(conv_ln_b),
                           ts=512, rows=32)
    return _tail(x, attn, conv, p, w_out[:a].astype(BF16), w_out[a:].astype(BF16), row(ffn_norm),
                 w_gate.astype(BF16), w_up.astype(BF16), w_down.astype(BF16), row(ple_norm),
                 w_ple_gate.astype(BF16), w_ple_proj.astype(BF16), tm=512)


def kernel(x, p, attn_norm, w_in, q_norm, k_norm, lambda_q1, lambda_k1, lambda_q2, lambda_k2,
           head_norm, conv_w, conv_b, conv_ln_g, conv_ln_b, w_out, ffn_norm, w_gate, w_up, w_down,
           ple_norm, w_ple_gate, w_ple_proj):
    depth = w_in.shape[0]
    batch = x.shape[0]
    outs = []
    for b in range(batch):
        xb = x[b]
        for i in range(depth):
            lam_init = 0.8 - 0.6 * math.exp(-0.3 * i)
            xb = _layer(xb, p[i, b], lam_init, attn_norm[i], w_in[i], q_norm[i], k_norm[i],
                        lambda_q1[i], lambda_k1[i], lambda_q2[i], lambda_k2[i], head_norm[i],
                        conv_w[i], conv_b[i], conv_ln_g[i], conv_ln_b[i], w_out[i], ffn_norm[i],
                        w_gate[i], w_up[i], w_down[i], ple_norm[i], w_ple_gate[i], w_ple_proj[i])
        outs.append(xb)
    return jnp.stack(outs, axis=0)
```

```python
import functools
import math

import numpy as np
import jax
import jax.numpy as jnp
from jax import lax
from jax.experimental import pallas as pl
from jax.experimental.pallas import tpu as pltpu

D_MODEL = 1024
ATTN_WIDTH = 512
CONV_WIDTH = 512
DIFF_HEAD_DIM = 64
V_HEAD_DIM = 128
V_ROWS = V_HEAD_DIM + 16
N_ATTN_HEADS = 4
CONV_KERNEL = 31
CONV_HALO = 16
SUBLANES = 8
FFN_HIDDEN = 2816
PLE_DIM = 256
EPS = 1e-6

VMEM_LIMIT_BYTES = 56 * 1024 * 1024

ROW_TILE = 512
CONV_ROW_BLOCK = 32
ATTN_Q_TILE = 1024
ATTN_K_CHUNK = 1024

F32 = jnp.float32
BF16 = jnp.bfloat16
NEG_BIG = -1e30
LOG2E = math.log2(math.e)


def _rmsnorm_rows(x, gain):
    ms = jnp.mean(x * x, axis=-1, keepdims=True)
    return x * lax.rsqrt(ms + EPS) * gain


def _dot(a, b):
    return jnp.dot(a, b, preferred_element_type=F32)


def _dot_nt(a, b):
    return lax.dot_general(a, b, (((1,), (1,)), ((), ())), preferred_element_type=F32)


def _inproj_kernel(x_ref, g_ref, wqk_ref, wvt_ref, wc_ref, qn_ref, kn_ref, seg_ref,
                   q_ref, k_ref, vt_ref, u_ref):
    h = _rmsnorm_rows(x_ref[...], g_ref[...]).astype(BF16)
    qk = _dot(h, wqk_ref[...])

    def seg_norm(y, gain):
        sq = y * y
        hi = sq.astype(BF16)
        lo = (sq - hi.astype(F32)).astype(BF16)
        ms = _dot(hi, seg_ref[...]) + _dot(lo, seg_ref[...])
        return y * lax.rsqrt(ms + EPS) * gain

    q = seg_norm(qk[:, :ATTN_WIDTH], qn_ref[...]) * (DIFF_HEAD_DIM ** -0.5 * LOG2E)
    k = seg_norm(qk[:, ATTN_WIDTH:], kn_ref[...])
    q_ref[...] = q.astype(BF16)
    k_ref[...] = k.astype(BF16)
    vt = _dot_nt(wvt_ref[...], h).astype(BF16)
    ones = jnp.ones((V_ROWS - V_HEAD_DIM, vt.shape[1]), BF16)
    for hd in range(N_ATTN_HEADS):
        vt_ref[hd * V_ROWS:hd * V_ROWS + V_HEAD_DIM, :] = vt[hd * V_HEAD_DIM:(hd + 1) * V_HEAD_DIM]
        vt_ref[hd * V_ROWS + V_HEAD_DIM:(hd + 1) * V_ROWS, :] = ones
    c = _dot(h, wc_ref[...])
    u_ref[...] = c[:, :CONV_WIDTH] * jax.nn.sigmoid(c[:, CONV_WIDTH:])


def _inproj(x, gain, wqk, wvt, wc, qn, kn, seg, *, tm):
    s = x.shape[0]
    const = lambda i: (0, 0)
    return pl.pallas_call(
        _inproj_kernel,
        grid=(s // tm,),
        in_specs=[
            pl.BlockSpec((tm, D_MODEL), lambda i: (i, 0)),
            pl.BlockSpec((1, D_MODEL), const),
            pl.BlockSpec((D_MODEL, 2 * ATTN_WIDTH), const),
            pl.BlockSpec((ATTN_WIDTH, D_MODEL), const),
            pl.BlockSpec((D_MODEL, 2 * CONV_WIDTH), const),
            pl.BlockSpec((1, ATTN_WIDTH), const),
            pl.BlockSpec((1, ATTN_WIDTH), const),
            pl.BlockSpec((ATTN_WIDTH, ATTN_WIDTH), const),
        ],
        out_specs=[
            pl.BlockSpec((tm, ATTN_WIDTH), lambda i: (i, 0)),
            pl.BlockSpec((tm, ATTN_WIDTH), lambda i: (i, 0)),
            pl.BlockSpec((N_ATTN_HEADS * V_ROWS, tm), lambda i: (0, i)),
            pl.BlockSpec((tm, CONV_WIDTH), lambda i: (i, 0)),
        ],
        out_shape=[
            jax.ShapeDtypeStruct((s, ATTN_WIDTH), BF16),
            jax.ShapeDtypeStruct((s, ATTN_WIDTH), BF16),
            jax.ShapeDtypeStruct((N_ATTN_HEADS * V_ROWS, s), BF16),
            jax.ShapeDtypeStruct((s, CONV_WIDTH), F32),
        ],
        compiler_params=pltpu.CompilerParams(
            dimension_semantics=("arbitrary",), vmem_limit_bytes=VMEM_LIMIT_BYTES),
        name="inproj",
    )(x, gain, wqk, wvt, wc, qn, kn, seg)


N_POS_SPLIT = 3
POS_HI_STEP = 16
BF16_EXACT_INT = 256
EXP_STRIP = 64


def _bf16_pieces(value, n):
    pieces, rest = [], np.float64(value)
    for _ in range(n):
        piece = np.float64(np.asarray(rest, dtype=BF16).astype(np.float64))
        pieces.append(piece)
        rest = rest - piece
    return pieces


def _alibi_tables(tq, tk):
    assert tk // POS_HI_STEP <= BF16_EXACT_INT and tq <= BF16_EXACT_INT * BF16_EXACT_INT
    pos = np.arange(tk)
    qidx = np.arange(tq)
    feat = np.zeros((N_ATTN_HEADS, tk, V_HEAD_DIM), np.float64)
    qpos = np.zeros((N_ATTN_HEADS, V_HEAD_DIM, tq), np.float64)
    n_lin = 2 * N_POS_SPLIT
    for h in range(N_ATTN_HEADS):
        c = 2.0 ** (-8.0 * (h + 1) / N_ATTN_HEADS) * LOG2E
        for s, piece in enumerate(_bf16_pieces(c, N_POS_SPLIT)):
            feat[h, :, 2 * s] = pos // POS_HI_STEP
            feat[h, :, 2 * s + 1] = pos % POS_HI_STEP
            qpos[h, 2 * s, :] = -POS_HI_STEP * piece
            qpos[h, 2 * s + 1, :] = -piece
            feat[h, :, n_lin + 2 * s] = piece
            feat[h, :, n_lin + 2 * s + 1] = piece
            qpos[h, n_lin + 2 * s, :] = qidx % BF16_EXACT_INT
            qpos[h, n_lin + 2 * s + 1, :] = qidx - qidx % BF16_EXACT_INT
    return jnp.asarray(feat, BF16), jnp.asarray(qpos, BF16)


def _attn_kernel(c_ref, q_ref, k_ref, vt_ref, feat_ref, qpos_ref,
                 lq1_ref, lk1_ref, lq2_ref, lk2_ref, hn_ref,
                 o_ref, qz_ref, acc1_ref, acc2_ref, s_even_ref, s_odd_ref, p_even_ref, p_odd_ref,
                 *, tq, tk, seq, lam_init):
    head = pl.program_id(0)
    qi = pl.program_id(1)
    c = c_ref[head]
    q0 = qi * tq

    qt = q_ref[...].astype(F32).T.astype(BF16)
    row = lax.broadcasted_iota(jnp.int32, qt.shape, 0)
    zero = jnp.zeros_like(qt)
    q1t = jnp.where(row < DIFF_HEAD_DIM, qt, zero)
    q2t = jnp.where(row >= DIFF_HEAD_DIM, qt, zero)
    pos_after = qpos_ref[...]
    for side, pos in ((0, -pos_after), (1, pos_after)):
        qz_ref[side, 0:V_HEAD_DIM, 0:tq] = q1t
        qz_ref[side, 0:V_HEAD_DIM, tq:2 * tq] = q2t
        qz_ref[side, V_HEAD_DIM:, 0:tq] = pos
        qz_ref[side, V_HEAD_DIM:, tq:2 * tq] = pos

    acc1_ref[...] = jnp.zeros_like(acc1_ref)
    acc2_ref[...] = jnp.zeros_like(acc2_ref)

    n_chunks = seq // tk
    assert n_chunks % 2 == 0 and tk % tq == 0
    diag = lax.div(q0, tk)
    s_bufs = (s_even_ref, s_odd_ref)
    p_bufs = (p_even_ref, p_odd_ref)

    def chunk_of(i):
        after = (i > diag).astype(jnp.int32)
        return i - 1 + after, after

    def key_rows(j):
        return k_ref[pl.ds(pl.multiple_of(j * tk, tk), tk), :]

    def scores(i, s_ref):
        j, after = chunk_of(i)
        s = _dot(jnp.concatenate([key_rows(j), feat_ref[...]], axis=1), qz_ref[after])
        s_ref[...] = s
        return jnp.max(s, axis=0, keepdims=True)

    def scores_diag(s_ref):
        s = _dot(key_rows(diag), jnp.concatenate([q1t, q2t], axis=1))
        rel = (lax.broadcasted_iota(jnp.int32, (tk, tq), 0)
               - lax.broadcasted_iota(jnp.int32, (tk, tq), 1)).astype(F32)
        bias = c * jnp.abs(rel + (diag * tk - q0).astype(F32))
        s = s - jnp.concatenate([bias, bias], axis=1)
        s_ref[...] = s
        return jnp.max(s, axis=0, keepdims=True)

    def chunk_offset(i):
        j, _ = chunk_of(i)
        return c * jnp.abs(j * tk - q0).astype(F32)

    def exps(s_ref, p_ref, cmax, off, m):
        m_new = jnp.maximum(m, cmax - off)
        shift = m_new + off
        for r in range(0, tk, EXP_STRIP):
            p_ref[r:r + EXP_STRIP, :] = jnp.exp2((s_ref[r:r + EXP_STRIP, :] - shift).astype(BF16))
        return m_new, jnp.exp2(m - m_new)

    def accumulate(p_ref, alpha, j):
        vtc = vt_ref[:, pl.ds(pl.multiple_of(j * tk, tk), tk)]
        acc1_ref[...] = alpha[:, :tq] * acc1_ref[...] + _dot(vtc, p_ref[:, :tq])
        acc2_ref[...] = alpha[:, tq:] * acc2_ref[...] + _dot(vtc, p_ref[:, tq:])

    def accumulate_visit(p_ref, alpha, i):
        accumulate(p_ref, alpha, jnp.where(i == 0, diag, chunk_of(i)[0]))

    m = jnp.full((1, 2 * tq), NEG_BIG, F32)
    cmax = scores_diag(s_bufs[0])
    m, alpha = exps(s_bufs[0], p_bufs[0], cmax, 0.0, m)
    cmax = scores(1, s_bufs[1])

    def body(t, carry):
        m, cmax, alpha = carry
        for parity in range(2):
            i = 2 * t + parity
            cmax_next = scores(i + 2, s_bufs[parity])
            m, alpha_next = exps(s_bufs[1 - parity], p_bufs[1 - parity], cmax, chunk_offset(i + 1), m)
            accumulate_visit(p_bufs[parity], alpha, i)
            cmax, alpha = cmax_next, alpha_next
        return m, cmax, alpha

    m, cmax, alpha = lax.fori_loop(0, (n_chunks - 2) // 2, body, (m, cmax, alpha))
    last = n_chunks - 1
    m, alpha_last = exps(s_bufs[1], p_bufs[1], cmax, chunk_offset(last), m)
    accumulate_visit(p_bufs[0], alpha, last - 1)
    accumulate_visit(p_bufs[1], alpha_last, last)

    lam = (jnp.exp(jnp.sum(lq1_ref[...] * lk1_ref[...], axis=-1, keepdims=True))
           - jnp.exp(jnp.sum(lq2_ref[...] * lk2_ref[...], axis=-1, keepdims=True))
           + lam_init)
    o_t = (acc1_ref[0:V_HEAD_DIM, :] / acc1_ref[V_HEAD_DIM:V_HEAD_DIM + 1, :]
           - lam * (acc2_ref[0:V_HEAD_DIM, :] / acc2_ref[V_HEAD_DIM:V_HEAD_DIM + 1, :]))
    o = o_t.T
    o_ref[...] = (_rmsnorm_rows(o, hn_ref[...]) * (1.0 - lam_init)).astype(o_ref.dtype)


def _attention(q, k, vt, lq1, lk1, lq2, lk2, hn, *, tq, tk, lam_init):
    s = q.shape[0]
    c = jnp.asarray([2.0 ** (-8.0 * (h + 1) / N_ATTN_HEADS) * LOG2E for h in range(N_ATTN_HEADS)], F32)
    feat, qpos = _alibi_tables(tq, tk)
    lam_spec = pl.BlockSpec((1, DIFF_HEAD_DIM), lambda h, i: (0, 0))
    kern = functools.partial(_attn_kernel, tq=tq, tk=tk, seq=s, lam_init=lam_init)
    return pl.pallas_call(
        kern,
        grid=(N_ATTN_HEADS, s // tq),
        in_specs=[
            pl.BlockSpec(memory_space=pltpu.SMEM),
            pl.BlockSpec((tq, V_HEAD_DIM), lambda h, i: (i, h)),
            pl.BlockSpec((s, V_HEAD_DIM), lambda h, i: (0, h)),
            pl.BlockSpec((V_ROWS, s), lambda h, i: (h, 0)),
            pl.BlockSpec((None, tk, V_HEAD_DIM), lambda h, i: (h, 0, 0)),
            pl.BlockSpec((None, V_HEAD_DIM, tq), lambda h, i: (h, 0, 0)),
            lam_spec, lam_spec, lam_spec, lam_spec,
            pl.BlockSpec((1, V_HEAD_DIM), lambda h, i: (0, 0)),
        ],
        out_specs=pl.BlockSpec((tq, V_HEAD_DIM), lambda h, i: (i, h)),
        out_shape=jax.ShapeDtypeStruct((s, ATTN_WIDTH), BF16),
        scratch_shapes=[pltpu.VMEM((2, 2 * V_HEAD_DIM, 2 * tq), BF16),
                        pltpu.VMEM((V_ROWS, tq), F32), pltpu.VMEM((V_ROWS, tq), F32),
                        pltpu.VMEM((tk, 2 * tq), F32), pltpu.VMEM((tk, 2 * tq), F32),
                        pltpu.VMEM((tk, 2 * tq), BF16), pltpu.VMEM((tk, 2 * tq), BF16)],
        compiler_params=pltpu.CompilerParams(
            dimension_semantics=("arbitrary", "arbitrary"), vmem_limit_bytes=VMEM_LIMIT_BYTES),
        name="diff_attention",
    )(c, q, k, vt, feat, qpos, lq1, lk1, lq2, lk2, hn)


def _conv_kernel(prev_ref, main_ref, next_ref, w_ref, b_ref, g_ref, beta_ref, o_ref, buf_ref,
                 shift_ref, *, ts, rows):
    i = pl.program_id(0)
    last = pl.num_programs(0) - 1
    buf_ref[0:CONV_HALO, :] = jnp.where(i > 0, prev_ref[...], 0.0)
    buf_ref[CONV_HALO:CONV_HALO + ts, :] = main_ref[...]
    buf_ref[CONV_HALO + ts:2 * CONV_HALO + ts, :] = jnp.where(i < last, next_ref[...], 0.0)
    n_shift = shift_ref.shape[1]
    for phase in range(1, SUBLANES):
        shift_ref[phase - 1] = buf_ref[phase:phase + n_shift, :]
    first_tap = CONV_HALO - CONV_KERNEL // 2
    for r0 in range(0, ts, rows):
        acc = jnp.broadcast_to(b_ref[...], (rows // SUBLANES, SUBLANES, CONV_WIDTH))
        for t in range(CONV_KERNEL):
            phase = (first_tap + t) % SUBLANES
            start = r0 + (first_tap + t) - phase
            src = buf_ref if phase == 0 else shift_ref.at[phase - 1]
            window = src[start:start + rows, :].reshape(rows // SUBLANES, SUBLANES, CONV_WIDTH)
            acc = acc + w_ref[t][None] * window
        acc = acc.reshape(rows, CONV_WIDTH)
        mu = jnp.mean(acc, axis=-1, keepdims=True)
        xc = acc - mu
        y = xc * lax.rsqrt(jnp.mean(xc * xc, axis=-1, keepdims=True) + EPS)
        y = y * g_ref[...] + beta_ref[...]
        o_ref[r0:r0 + rows, :] = (y * jax.nn.sigmoid(y)).astype(o_ref.dtype)


def _conformer_conv(u, w, b, g, beta, *, ts, rows):
    s = u.shape[0]
    nh = ts // CONV_HALO
    n_halo_blocks = s // CONV_HALO
    const = lambda i: (0, 0)
    kern = functools.partial(_conv_kernel, ts=ts, rows=rows)
    return pl.pallas_call(
        kern,
        grid=(s // ts,),
        in_specs=[
            pl.BlockSpec((CONV_HALO, CONV_WIDTH), lambda i: (jnp.maximum(i * nh - 1, 0), 0)),
            pl.BlockSpec((ts, CONV_WIDTH), lambda i: (i, 0)),
            pl.BlockSpec((CONV_HALO, CONV_WIDTH),
                         lambda i: (jnp.minimum((i + 1) * nh, n_halo_blocks - 1), 0)),
            pl.BlockSpec((CONV_KERNEL, SUBLANES, CONV_WIDTH), lambda i: (0, 0, 0)),
            pl.BlockSpec((1, CONV_WIDTH), const),
            pl.BlockSpec((1, CONV_WIDTH), const),
            pl.BlockSpec((1, CONV_WIDTH), const),
        ],
        out_specs=pl.BlockSpec((ts, CONV_WIDTH), lambda i: (i, 0)),
        out_shape=jax.ShapeDtypeStruct((s, CONV_WIDTH), BF16),
        scratch_shapes=[pltpu.VMEM((ts + 2 * CONV_HALO, CONV_WIDTH), F32),
                        pltpu.VMEM((SUBLANES - 1, ts + 2 * CONV_HALO - SUBLANES, CONV_WIDTH), F32)],
        compiler_params=pltpu.CompilerParams(
            dimension_semantics=("arbitrary",), vmem_limit_bytes=VMEM_LIMIT_BYTES),
        name="conformer_conv",
    )(u, u, u, w, b, g, beta)


FFN_CHUNKS = ((0, 1024), (1024, 2048), (2048, FFN_HIDDEN))


def _tail_kernel(x_ref, a_ref, c_ref, p_ref, woa_ref, woc_ref, fg_ref, wg_ref, wu_ref, wd_ref,
                 pg_ref, wpg_ref, wpp_ref, o_ref, act_ref):
    x1 = x_ref[...] + _dot(a_ref[...], woa_ref[...]) + _dot(c_ref[...], woc_ref[...])
    h = _rmsnorm_rows(x1, fg_ref[...]).astype(BF16)
    for c0, c1 in FFN_CHUNKS:
        g = _dot(h, wg_ref[:, c0:c1])
        u = _dot(h, wu_ref[:, c0:c1])
        act_ref[:, c0:c1] = (g * jax.nn.sigmoid(g) * u).astype(BF16)
    x2 = x1 + _dot(act_ref[...], wd_ref[...])
    hp = _rmsnorm_rows(x2, pg_ref[...]).astype(BF16)
    gate = jax.nn.sigmoid(_dot(hp, wpg_ref[...]))
    o_ref[...] = x2 + gate * _dot(p_ref[...].astype(BF16), wpp_ref[...])


def _tail(x, a, c, p, woa, woc, fg, wg, wu, wd, pg, wpg, wpp, *, tm):
    s = x.shape[0]
    const = lambda i: (0, 0)

    def resident(shape):
        return pl.BlockSpec(shape, const, pipeline_mode=pl.Buffered(1))

    return pl.pallas_call(
        _tail_kernel,
        grid=(s // tm,),
        in_specs=[
            pl.BlockSpec((tm, D_MODEL), lambda i: (i, 0)),
            pl.BlockSpec((tm, ATTN_WIDTH), lambda i: (i, 0)),
            pl.BlockSpec((tm, CONV_WIDTH), lambda i: (i, 0)),
            pl.BlockSpec((tm, PLE_DIM), lambda i: (i, 0)),
            resident((ATTN_WIDTH, D_MODEL)),
            resident((CONV_WIDTH, D_MODEL)),
            resident((1, D_MODEL)),
            resident((D_MODEL, FFN_HIDDEN)),
            resident((D_MODEL, FFN_HIDDEN)),
            resident((FFN_HIDDEN, D_MODEL)),
            resident((1, D_MODEL)),
            resident((D_MODEL, D_MODEL)),
            resident((PLE_DIM, D_MODEL)),
        ],
        out_specs=pl.BlockSpec((tm, D_MODEL), lambda i: (i, 0)),
        out_shape=jax.ShapeDtypeStruct((s, D_MODEL), F32),
        scratch_shapes=[pltpu.VMEM((tm, FFN_HIDDEN), BF16)],
        compiler_params=pltpu.CompilerParams(
            dimension_semantics=("arbitrary",), vmem_limit_bytes=VMEM_LIMIT_BYTES),
        name="tail",
    )(x, a, c, p, woa, woc, fg, wg, wu, wd, pg, wpg, wpp)


def _layer(x, p, lam_init, attn_norm, w_in, q_norm, k_norm, lq1, lk1, lq2, lk2, head_norm,
           conv_w, conv_b, conv_ln_g, conv_ln_b, w_out, ffn_norm, w_gate, w_up, w_down,
           ple_norm, w_ple_gate, w_ple_proj):
    row = lambda v: v.reshape(1, -1).astype(F32)
    a = ATTN_WIDTH
    wqk = w_in[:, :2 * a].astype(BF16)
    wvt = w_in[:, 2 * a:3 * a].T.astype(BF16)
    wc = w_in[:, 3 * a:].astype(BF16)
    n_seg = ATTN_WIDTH // DIFF_HEAD_DIM
    seg = jnp.asarray(np.kron(np.eye(n_seg), np.full((DIFF_HEAD_DIM, DIFF_HEAD_DIM),
                                                      1.0 / DIFF_HEAD_DIM)), BF16)
    qn = row(jnp.tile(q_norm, n_seg))
    kn = row(jnp.tile(k_norm, n_seg))
    q, k, vt, u = _inproj(x, row(attn_norm), wqk, wvt, wc, qn, kn, seg, tm=2 * ROW_TILE)

    attn = _attention(q, k, vt, row(lq1), row(lk1), row(lq2), row(lk2), row(head_norm),
                      tq=ATTN_Q_TILE, tk=ATTN_K_CHUNK, lam_init=lam_init)
    conv_taps = jnp.broadcast_to(conv_w.astype(F32)[:, None, :], (CONV_KERNEL, SUBLANES, CONV_WIDTH))
    conv = _conformer_conv(u, conv_taps, row(conv_b), row(conv_ln_g), row(conv_ln_b),
                           ts=ROW_TILE, rows=CONV_ROW_BLOCK)
    return _tail(x, attn, conv, p, w_out[:a].astype(BF16), w_out[a:].astype(BF16), row(ffn_norm),
                 w_gate.astype(BF16), w_up.astype(BF16), w_down.astype(BF16), row(ple_norm),
                 w_ple_gate.astype(BF16), w_ple_proj.astype(BF16), tm=ROW_TILE)


def kernel(x, p, attn_norm, w_in, q_norm, k_norm, lambda_q1, lambda_k1, lambda_q2, lambda_k2,
           head_norm, conv_w, conv_b, conv_ln_g, conv_ln_b, w_out, ffn_norm, w_gate, w_up, w_down,
           ple_norm, w_ple_gate, w_ple_proj):
    depth = w_in.shape[0]
    batch = x.shape[0]
    outs = []
    for b in range(batch):
        xb = x[b]
        for i in range(depth):
            lam_init = 0.8 - 0.6 * math.exp(-0.3 * i)
            xb = _layer(xb, p[i, b], lam_init, attn_norm[i], w_in[i], q_norm[i], k_norm[i],
                        lambda_q1[i], lambda_k1[i], lambda_q2[i], lambda_k2[i], head_norm[i],
                        conv_w[i], conv_b[i], conv_ln_g[i], conv_ln_b[i], w_out[i], ffn_norm[i],
                        w_gate[i], w_up[i], w_down[i], ple_norm[i], w_ple_gate[i], w_ple_proj[i])
        outs.append(xb)
    return jnp.stack(outs, axis=0)
```

```python
import functools
import math

import numpy as np
import jax
import jax.numpy as jnp
from jax import lax
from jax.experimental import pallas as pl
from jax.experimental.pallas import tpu as pltpu

D_MODEL = 1024
ATTN_WIDTH = 512
CONV_WIDTH = 512
DIFF_HEAD_DIM = 64
V_HEAD_DIM = 128
V_ROWS = V_HEAD_DIM + 16
N_ATTN_HEADS = 4
CONV_KERNEL = 31
CONV_HALO = 16
SUBLANES = 8
FFN_HIDDEN = 2816
PLE_DIM = 256
EPS = 1e-6

VMEM_LIMIT_BYTES = 56 * 1024 * 1024

ROW_TILE = 512
CONV_ROW_BLOCK = 32
ATTN_Q_TILE = 1024
ATTN_K_CHUNK = 1024

F32 = jnp.float32
BF16 = jnp.bfloat16
NEG_BIG = -1e30
LOG2E = math.log2(math.e)


def _rmsnorm_rows(x, gain):
    ms = jnp.mean(x * x, axis=-1, keepdims=True)
    return x * lax.rsqrt(ms + EPS) * gain


def _dot(a, b):
    return jnp.dot(a, b, preferred_element_type=F32)


def _dot_nt(a, b):
    return lax.dot_general(a, b, (((1,), (1,)), ((), ())), preferred_element_type=F32)


def _inproj_kernel(x_ref, g_ref, wqk_ref, wvt_ref, wc_ref, qn_ref, kn_ref, seg_ref,
                   q_ref, k_ref, vt_ref, u_ref):
    h = _rmsnorm_rows(x_ref[...], g_ref[...]).astype(BF16)
    qk = _dot(h, wqk_ref[...])

    def seg_norm(y, gain):
        ms = _dot((y * y).astype(BF16), seg_ref[...])
        return y * lax.rsqrt(ms + EPS) * gain

    q = seg_norm(qk[:, :ATTN_WIDTH], qn_ref[...]) * (DIFF_HEAD_DIM ** -0.5 * LOG2E)
    k = seg_norm(qk[:, ATTN_WIDTH:], kn_ref[...])
    q_ref[...] = q.astype(BF16)
    k_ref[...] = k.astype(BF16)
    vt = _dot_nt(wvt_ref[...], h).astype(BF16)
    ones = jnp.ones((V_ROWS - V_HEAD_DIM, vt.shape[1]), BF16)
    for hd in range(N_ATTN_HEADS):
        vt_ref[hd * V_ROWS:hd * V_ROWS + V_HEAD_DIM, :] = vt[hd * V_HEAD_DIM:(hd + 1) * V_HEAD_DIM]
        vt_ref[hd * V_ROWS + V_HEAD_DIM:(hd + 1) * V_ROWS, :] = ones
    c = _dot(h, wc_ref[...])
    u_ref[...] = c[:, :CONV_WIDTH] * jax.nn.sigmoid(c[:, CONV_WIDTH:])


def _inproj(x, gain, wqk, wvt, wc, qn, kn, seg, *, tm):
    s = x.shape[0]
    const = lambda i: (0, 0)
    return pl.pallas_call(
        _inproj_kernel,
        grid=(s // tm,),
        in_specs=[
            pl.BlockSpec((tm, D_MODEL), lambda i: (i, 0)),
            pl.BlockSpec((1, D_MODEL), const),
            pl.BlockSpec((D_MODEL, 2 * ATTN_WIDTH), const),
            pl.BlockSpec((ATTN_WIDTH, D_MODEL), const),
            pl.BlockSpec((D_MODEL, 2 * CONV_WIDTH), const),
            pl.BlockSpec((1, ATTN_WIDTH), const),
            pl.BlockSpec((1, ATTN_WIDTH), const),
            pl.BlockSpec((ATTN_WIDTH, ATTN_WIDTH), const),
        ],
        out_specs=[
            pl.BlockSpec((tm, ATTN_WIDTH), lambda i: (i, 0)),
            pl.BlockSpec((tm, ATTN_WIDTH), lambda i: (i, 0)),
            pl.BlockSpec((N_ATTN_HEADS * V_ROWS, tm), lambda i: (0, i)),
            pl.BlockSpec((tm, CONV_WIDTH), lambda i: (i, 0)),
        ],
        out_shape=[
            jax.ShapeDtypeStruct((s, ATTN_WIDTH), BF16),
            jax.ShapeDtypeStruct((s, ATTN_WIDTH), BF16),
            jax.ShapeDtypeStruct((N_ATTN_HEADS * V_ROWS, s), BF16),
            jax.ShapeDtypeStruct((s, CONV_WIDTH), F32),
        ],
        compiler_params=pltpu.CompilerParams(
            dimension_semantics=("arbitrary",), vmem_limit_bytes=VMEM_LIMIT_BYTES),
        name="inproj",
    )(x, gain, wqk, wvt, wc, qn, kn, seg)


N_POS_SPLIT = 3
POS_HI_STEP = 16
BF16_EXACT_INT = 256
EXP_STRIP = 64


def _bf16_pieces(value, n):
    pieces, rest = [], np.float64(value)
    for _ in range(n):
        piece = np.float64(np.asarray(rest, dtype=BF16).astype(np.float64))
        pieces.append(piece)
        rest = rest - piece
    return pieces


def _alibi_tables(tq, tk):
    assert tk // POS_HI_STEP <= BF16_EXACT_INT and tq <= BF16_EXACT_INT * BF16_EXACT_INT
    pos = np.arange(tk)
    qidx = np.arange(tq)
    feat = np.zeros((N_ATTN_HEADS, tk, V_HEAD_DIM), np.float64)
    qpos = np.zeros((N_ATTN_HEADS, V_HEAD_DIM, tq), np.float64)
    n_lin = 2 * N_POS_SPLIT
    for h in range(N_ATTN_HEADS):
        c = 2.0 ** (-8.0 * (h + 1) / N_ATTN_HEADS) * LOG2E
        for s, piece in enumerate(_bf16_pieces(c, N_POS_SPLIT)):
            feat[h, :, 2 * s] = pos // POS_HI_STEP
            feat[h, :, 2 * s + 1] = pos % POS_HI_STEP
            qpos[h, 2 * s, :] = -POS_HI_STEP * piece
            qpos[h, 2 * s + 1, :] = -piece
            feat[h, :, n_lin + 2 * s] = piece
            feat[h, :, n_lin + 2 * s + 1] = piece
            qpos[h, n_lin + 2 * s, :] = qidx % BF16_EXACT_INT
            qpos[h, n_lin + 2 * s + 1, :] = qidx - qidx % BF16_EXACT_INT
    return jnp.asarray(feat, BF16), jnp.asarray(qpos, BF16)


def _attn_kernel(c_ref, q_ref, k_ref, vt_ref, feat_ref, qpos_ref,
                 lq1_ref, lk1_ref, lq2_ref, lk2_ref, hn_ref,
                 o_ref, qz_ref, acc1_ref, acc2_ref, s_even_ref, s_odd_ref, p_even_ref, p_odd_ref,
                 *, tq, tk, seq, lam_init):
    head = pl.program_id(0)
    qi = pl.program_id(1)
    c = c_ref[head]
    q0 = qi * tq

    qt = q_ref[...].astype(F32).T.astype(BF16)
    row = lax.broadcasted_iota(jnp.int32, qt.shape, 0)
    zero = jnp.zeros_like(qt)
    q1t = jnp.where(row < DIFF_HEAD_DIM, qt, zero)
    q2t = jnp.where(row >= DIFF_HEAD_DIM, qt, zero)
    pos_after = qpos_ref[...]
    for side, pos in ((0, -pos_after), (1, pos_after)):
        qz_ref[side, 0:V_HEAD_DIM, 0:tq] = q1t
        qz_ref[side, 0:V_HEAD_DIM, tq:2 * tq] = q2t
        qz_ref[side, V_HEAD_DIM:, 0:tq] = pos
        qz_ref[side, V_HEAD_DIM:, tq:2 * tq] = pos

    acc1_ref[...] = jnp.zeros_like(acc1_ref)
    acc2_ref[...] = jnp.zeros_like(acc2_ref)

    n_chunks = seq // tk
    assert n_chunks % 2 == 0 and tk % tq == 0
    diag = lax.div(q0, tk)
    s_bufs = (s_even_ref, s_odd_ref)
    p_bufs = (p_even_ref, p_odd_ref)

    def chunk_of(i):
        after = (i > diag).astype(jnp.int32)
        return i - 1 + after, after

    def key_rows(j):
        return k_ref[pl.ds(pl.multiple_of(j * tk, tk), tk), :]

    def scores(i, s_ref):
        j, after = chunk_of(i)
        s = _dot(jnp.concatenate([key_rows(j), feat_ref[...]], axis=1), qz_ref[after])
        s_ref[...] = s
        return jnp.max(s, axis=0, keepdims=True)

    def scores_diag(s_ref):
        s = _dot(key_rows(diag), jnp.concatenate([q1t, q2t], axis=1))
        rel = (lax.broadcasted_iota(jnp.int32, (tk, tq), 0)
               - lax.broadcasted_iota(jnp.int32, (tk, tq), 1)).astype(F32)
        bias = c * jnp.abs(rel + (diag * tk - q0).astype(F32))
        s = s - jnp.concatenate([bias, bias], axis=1)
        s_ref[...] = s
        return jnp.max(s, axis=0, keepdims=True)

    def chunk_offset(i):
        j, _ = chunk_of(i)
        return c * jnp.abs(j * tk - q0).astype(F32)

    def exps(s_ref, p_ref, cmax, off, m):
        m_new = jnp.maximum(m, cmax - off)
        shift = m_new + off
        for r in range(0, tk, EXP_STRIP):
            p_ref[r:r + EXP_STRIP, :] = jnp.exp2((s_ref[r:r + EXP_STRIP, :] - shift).astype(BF16))
        return m_new, jnp.exp2(m - m_new)

    def accumulate(p_ref, alpha, j):
        vtc = vt_ref[:, pl.ds(pl.multiple_of(j * tk, tk), tk)]
        acc1_ref[...] = alpha[:, :tq] * acc1_ref[...] + _dot(vtc, p_ref[:, :tq])
        acc2_ref[...] = alpha[:, tq:] * acc2_ref[...] + _dot(vtc, p_ref[:, tq:])

    def accumulate_visit(p_ref, alpha, i):
        accumulate(p_ref, alpha, jnp.where(i == 0, diag, chunk_of(i)[0]))

    cmax_diag = scores_diag(s_bufs[0])
    cmax = scores(1, s_bufs[1])
    m = jnp.minimum(cmax, NEG_BIG)
    m, alpha = exps(s_bufs[0], p_bufs[0], cmax_diag, 0.0, m)

    def body(t, carry):
        m, cmax, alpha = carry
        for parity in range(2):
            i = 2 * t + parity
            cmax_next = scores(i + 2, s_bufs[parity])
            m, alpha_next = exps(s_bufs[1 - parity], p_bufs[1 - parity], cmax, chunk_offset(i + 1), m)
            accumulate_visit(p_bufs[parity], alpha, i)
            cmax, alpha = cmax_next, alpha_next
        return m, cmax, alpha

    m, cmax, alpha = lax.fori_loop(0, (n_chunks - 2) // 2, body, (m, cmax, alpha))
    last = n_chunks - 1
    m, alpha_last = exps(s_bufs[1], p_bufs[1], cmax, chunk_offset(last), m)
    accumulate_visit(p_bufs[0], alpha, last - 1)
    accumulate_visit(p_bufs[1], alpha_last, last)

    lam = (jnp.exp(jnp.sum(lq1_ref[...] * lk1_ref[...], axis=-1, keepdims=True))
           - jnp.exp(jnp.sum(lq2_ref[...] * lk2_ref[...], axis=-1, keepdims=True))
           + lam_init)
    o_t = (acc1_ref[0:V_HEAD_DIM, :] / acc1_ref[V_HEAD_DIM:V_HEAD_DIM + 1, :]
           - lam * (acc2_ref[0:V_HEAD_DIM, :] / acc2_ref[V_HEAD_DIM:V_HEAD_DIM + 1, :]))
    o = o_t.T
    o_ref[...] = (_rmsnorm_rows(o, hn_ref[...]) * (1.0 - lam_init)).astype(o_ref.dtype)


def _attention(q, k, vt, lq1, lk1, lq2, lk2, hn, *, tq, tk, lam_init):
    s = q.shape[0]
    c = jnp.asarray([2.0 ** (-8.0 * (h + 1) / N_ATTN_HEADS) * LOG2E for h in range(N_ATTN_HEADS)], F32)
    feat, qpos = _alibi_tables(tq, tk)
    lam_spec = pl.BlockSpec((1, DIFF_HEAD_DIM), lambda h, i: (0, 0))
    kern = functools.partial(_attn_kernel, tq=tq, tk=tk, seq=s, lam_init=lam_init)
    return pl.pallas_call(
        kern,
        grid=(N_ATTN_HEADS, s // tq),
        in_specs=[
            pl.BlockSpec(memory_space=pltpu.SMEM),
            pl.BlockSpec((tq, V_HEAD_DIM), lambda h, i: (i, h)),
            pl.BlockSpec((s, V_HEAD_DIM), lambda h, i: (0, h)),
            pl.BlockSpec((V_ROWS, s), lambda h, i: (h, 0)),
            pl.BlockSpec((None, tk, V_HEAD_DIM), lambda h, i: (h, 0, 0)),
            pl.BlockSpec((None, V_HEAD_DIM, tq), lambda h, i: (h, 0, 0)),
            lam_spec, lam_spec, lam_spec, lam_spec,
            pl.BlockSpec((1, V_HEAD_DIM), lambda h, i: (0, 0)),
        ],
        out_specs=pl.BlockSpec((tq, V_HEAD_DIM), lambda h, i: (i, h)),
        out_shape=jax.ShapeDtypeStruct((s, ATTN_WIDTH), BF16),
        scratch_shapes=[pltpu.VMEM((2, 2 * V_HEAD_DIM, 2 * tq), BF16),
                        pltpu.VMEM((V_ROWS, tq), F32), pltpu.VMEM((V_ROWS, tq), F32),
                        pltpu.VMEM((tk, 2 * tq), F32), pltpu.VMEM((tk, 2 * tq), F32),
                        pltpu.VMEM((tk, 2 * tq), BF16), pltpu.VMEM((tk, 2 * tq), BF16)],
        compiler_params=pltpu.CompilerParams(
            dimension_semantics=("arbitrary", "arbitrary"), vmem_limit_bytes=VMEM_LIMIT_BYTES),
        name="diff_attention",
    )(c, q, k, vt, feat, qpos, lq1, lk1, lq2, lk2, hn)


def _conv_kernel(prev_ref, main_ref, next_ref, w_ref, b_ref, g_ref, beta_ref, o_ref, buf_ref,
                 shift_ref, *, ts, rows):
    i = pl.program_id(0)
    last = pl.num_programs(0) - 1
    buf_ref[0:CONV_HALO, :] = jnp.where(i > 0, prev_ref[...], 0.0)
    buf_ref[CONV_HALO:CONV_HALO + ts, :] = main_ref[...]
    buf_ref[CONV_HALO + ts:2 * CONV_HALO + ts, :] = jnp.where(i < last, next_ref[...], 0.0)
    n_shift = shift_ref.shape[1]
    for phase in range(1, SUBLANES):
        shift_ref[phase - 1] = buf_ref[phase:phase + n_shift, :]
    first_tap = CONV_HALO - CONV_KERNEL // 2
    for r0 in range(0, ts, rows):
        acc = jnp.broadcast_to(b_ref[...], (rows // SUBLANES, SUBLANES, CONV_WIDTH))
        for t in range(CONV_KERNEL):
            phase = (first_tap + t) % SUBLANES
            start = r0 + (first_tap + t) - phase
            src = buf_ref if phase == 0 else shift_ref.at[phase - 1]
            window = src[start:start + rows, :].reshape(rows // SUBLANES, SUBLANES, CONV_WIDTH)
            acc = acc + w_ref[t][None] * window
        acc = acc.reshape(rows, CONV_WIDTH)
        mu = jnp.mean(acc, axis=-1, keepdims=True)
        xc = acc - mu
        y = xc * lax.rsqrt(jnp.mean(xc * xc, axis=-1, keepdims=True) + EPS)
        y = y * g_ref[...] + beta_ref[...]
        o_ref[r0:r0 + rows, :] = (y * jax.nn.sigmoid(y)).astype(o_ref.dtype)


def _conformer_conv(u, w, b, g, beta, *, ts, rows):
    s = u.shape[0]
    nh = ts // CONV_HALO
    n_halo_blocks = s // CONV_HALO
    const = lambda i: (0, 0)
    kern = functools.partial(_conv_kernel, ts=ts, rows=rows)
    return pl.pallas_call(
        kern,
        grid=(s // ts,),
        in_specs=[
            pl.BlockSpec((CONV_HALO, CONV_WIDTH), lambda i: (jnp.maximum(i * nh - 1, 0), 0)),
            pl.BlockSpec((ts, CONV_WIDTH), lambda i: (i, 0)),
            pl.BlockSpec((CONV_HALO, CONV_WIDTH),
                         lambda i: (jnp.minimum((i + 1) * nh, n_halo_blocks - 1), 0)),
            pl.BlockSpec((CONV_KERNEL, SUBLANES, CONV_WIDTH), lambda i: (0, 0, 0)),
            pl.BlockSpec((1, CONV_WIDTH), const),
            pl.BlockSpec((1, CONV_WIDTH), const),
            pl.BlockSpec((1, CONV_WIDTH), const),
        ],
        out_specs=pl.BlockSpec((ts, CONV_WIDTH), lambda i: (i, 0)),
        out_shape=jax.ShapeDtypeStruct((s, CONV_WIDTH), BF16),
        scratch_shapes=[pltpu.VMEM((ts + 2 * CONV_HALO, CONV_WIDTH), F32),
                        pltpu.VMEM((SUBLANES - 1, ts + 2 * CONV_HALO - SUBLANES, CONV_WIDTH), F32)],
        compiler_params=pltpu.CompilerParams(
            dimension_semantics=("arbitrary",), vmem_limit_bytes=VMEM_LIMIT_BYTES),
        name="conformer_conv",
    )(u, u, u, w, b, g, beta)


FFN_CHUNKS = ((0, 1024), (1024, 2048), (2048, FFN_HIDDEN))


def _tail_kernel(x_ref, a_ref, c_ref, p_ref, woa_ref, woc_ref, fg_ref, wg_ref, wu_ref, wd_ref,
                 pg_ref, wpg_ref, wpp_ref, o_ref, act_ref):
    x1 = x_ref[...] + _dot(a_ref[...], woa_ref[...]) + _dot(c_ref[...], woc_ref[...])
    h = _rmsnorm_rows(x1, fg_ref[...]).astype(BF16)
    for c0, c1 in FFN_CHUNKS:
        g = _dot(h, wg_ref[:, c0:c1])
        u = _dot(h, wu_ref[:, c0:c1])
        act_ref[:, c0:c1] = (g * jax.nn.sigmoid(g) * u).astype(BF16)
    x2 = x1 + _dot(act_ref[...], wd_ref[...])
    hp = _rmsnorm_rows(x2, pg_ref[...]).astype(BF16)
    gate = jax.nn.sigmoid(_dot(hp, wpg_ref[...]))
    o_ref[...] = x2 + gate * _dot(p_ref[...].astype(BF16), wpp_ref[...])


def _tail(x, a, c, p, woa, woc, fg, wg, wu, wd, pg, wpg, wpp, *, tm):
    s = x.shape[0]
    const = lambda i: (0, 0)

    def resident(shape):
        return pl.BlockSpec(shape, const, pipeline_mode=pl.Buffered(1))

    return pl.pallas_call(
        _tail_kernel,
        grid=(s // tm,),
        in_specs=[
            pl.BlockSpec((tm, D_MODEL), lambda i: (i, 0)),
            pl.BlockSpec((tm, ATTN_WIDTH), lambda i: (i, 0)),
            pl.BlockSpec((tm, CONV_WIDTH), lambda i: (i, 0)),
            pl.BlockSpec((tm, PLE_DIM), lambda i: (i, 0)),
            resident((ATTN_WIDTH, D_MODEL)),
            resident((CONV_WIDTH, D_MODEL)),
            resident((1, D_MODEL)),
            resident((D_MODEL, FFN_HIDDEN)),
            resident((D_MODEL, FFN_HIDDEN)),
            resident((FFN_HIDDEN, D_MODEL)),
            resident((1, D_MODEL)),
            resident((D_MODEL, D_MODEL)),
            resident((PLE_DIM, D_MODEL)),
        ],
        out_specs=pl.BlockSpec((tm, D_MODEL), lambda i: (i, 0)),
        out_shape=jax.ShapeDtypeStruct((s, D_MODEL), F32),
        scratch_shapes=[pltpu.VMEM((tm, FFN_HIDDEN), BF16)],
        compiler_params=pltpu.CompilerParams(
            dimension_semantics=("arbitrary",), vmem_limit_bytes=VMEM_LIMIT_BYTES),
        name="tail",
    )(x, a, c, p, woa, woc, fg, wg, wu, wd, pg, wpg, wpp)


def _layer(x, p, lam_init, attn_norm, w_in, q_norm, k_norm, lq1, lk1, lq2, lk2, head_norm,
           conv_w, conv_b, conv_ln_g, conv_ln_b, w_out, ffn_norm, w_gate, w_up, w_down,
           ple_norm, w_ple_gate, w_ple_proj):
    row = lambda v: v.reshape(1, -1).astype(F32)
    a = ATTN_WIDTH
    wqk = w_in[:, :2 * a].astype(BF16)
    wvt = w_in[:, 2 * a:3 * a].T.astype(BF16)
    wc = w_in[:, 3 * a:].astype(BF16)
    n_seg = ATTN_WIDTH // DIFF_HEAD_DIM
    seg = jnp.asarray(np.kron(np.eye(n_seg), np.full((DIFF_HEAD_DIM, DIFF_HEAD_DIM),
                                                      1.0 / DIFF_HEAD_DIM)), BF16)
    qn = row(jnp.tile(q_norm, n_seg))
    kn = row(jnp.tile(k_norm, n_seg))
    q, k, vt, u = _inproj(x, row(attn_norm), wqk, wvt, wc, qn, kn, seg, tm=2 * ROW_TILE)

    attn = _attention(q, k, vt, row(lq1), row(lk1), row(lq2), row(lk2), row(head_norm),
                      tq=ATTN_Q_TILE, tk=ATTN_K_CHUNK, lam_init=lam_init)
    conv_taps = jnp.broadcast_to(conv_w.astype(F32)[:, None, :], (CONV_KERNEL, SUBLANES, CONV_WIDTH))
    conv = _conformer_conv(u, conv_taps, row(conv_b), row(conv_ln_g), row(conv_ln_b),
                           ts=2 * ROW_TILE, rows=CONV_ROW_BLOCK)
    return _tail(x, attn, conv, p, w_out[:a].astype(BF16), w_out[a:].astype(BF16), row(ffn_norm),
                 w_gate.astype(BF16), w_up.astype(BF16), w_down.astype(BF16), row(ple_norm),
                 w_ple_gate.astype(BF16), w_ple_proj.astype(BF16), tm=ROW_TILE)


def kernel(x, p, attn_norm, w_in, q_norm, k_norm, lambda_q1, lambda_k1, lambda_q2, lambda_k2,
           head_norm, conv_w, conv_b, conv_ln_g, conv_ln_b, w_out, ffn_norm, w_gate, w_up, w_down,
           ple_norm, w_ple_gate, w_ple_proj):
    depth = w_in.shape[0]
    batch = x.shape[0]
    outs = []
    for b in range(batch):
        xb = x[b]
        for i in range(depth):
            lam_init = 0.8 - 0.6 * math.exp(-0.3 * i)
            xb = _layer(xb, p[i, b], lam_init, attn_norm[i], w_in[i], q_norm[i], k_norm[i],
                        lambda_q1[i], lambda_k1[i], lambda_q2[i], lambda_k2[i], head_norm[i],
                        conv_w[i], conv_b[i], conv_ln_g[i], conv_ln_b[i], w_out[i], ffn_norm[i],
                        w_gate[i], w_up[i], w_down[i], ple_norm[i], w_ple_gate[i], w_ple_proj[i])
        outs.append(xb)
    return jnp.stack(outs, axis=0)
```

```python
import functools
import math

import numpy as np
import jax
import jax.numpy as jnp
from jax import lax
from jax.experimental import pallas as pl
from jax.experimental.pallas import tpu as pltpu

D_MODEL = 1024
ATTN_WIDTH = 512
CONV_WIDTH = 512
DIFF_HEAD_DIM = 64
V_HEAD_DIM = 128
V_ROWS = V_HEAD_DIM + 16
N_ATTN_HEADS = 4
CONV_KERNEL = 31
CONV_HALO = 16
SUBLANES = 8
FFN_HIDDEN = 2816
PLE_DIM = 256
EPS = 1e-6

VMEM_LIMIT_BYTES = 56 * 1024 * 1024

ROW_TILE = 512
CONV_ROW_BLOCK = 32
ATTN_Q_TILE = 1024
ATTN_K_CHUNK = 1024

F32 = jnp.float32
BF16 = jnp.bfloat16
NEG_BIG = -1e30
LOG2E = math.log2(math.e)


def _rmsnorm_rows(x, gain):
    ms = jnp.mean(x * x, axis=-1, keepdims=True)
    return x * lax.rsqrt(ms + EPS) * gain


def _dot(a, b):
    return jnp.dot(a, b, preferred_element_type=F32)


def _dot_nt(a, b):
    return lax.dot_general(a, b, (((1,), (1,)), ((), ())), preferred_element_type=F32)


def _inproj_kernel(x_ref, g_ref, wqk_ref, wvt_ref, wc_ref, qn_ref, kn_ref, seg_ref,
                   q_ref, k_ref, vt_ref, u_ref):
    h = _rmsnorm_rows(x_ref[...], g_ref[...]).astype(BF16)
    qk = _dot(h, wqk_ref[...])

    def seg_norm(y, gain):
        ms = _dot((y * y).astype(BF16), seg_ref[...])
        return y * lax.rsqrt(ms + EPS) * gain

    q = seg_norm(qk[:, :ATTN_WIDTH], qn_ref[...]) * (DIFF_HEAD_DIM ** -0.5 * LOG2E)
    k = seg_norm(qk[:, ATTN_WIDTH:], kn_ref[...])
    q_ref[...] = q.astype(BF16)
    k_ref[...] = k.astype(BF16)
    vt = _dot_nt(wvt_ref[...], h).astype(BF16)
    ones = jnp.ones((V_ROWS - V_HEAD_DIM, vt.shape[1]), BF16)
    for hd in range(N_ATTN_HEADS):
        vt_ref[hd * V_ROWS:hd * V_ROWS + V_HEAD_DIM, :] = vt[hd * V_HEAD_DIM:(hd + 1) * V_HEAD_DIM]
        vt_ref[hd * V_ROWS + V_HEAD_DIM:(hd + 1) * V_ROWS, :] = ones
    c = _dot(h, wc_ref[...])
    u_ref[...] = c[:, :CONV_WIDTH] * jax.nn.sigmoid(c[:, CONV_WIDTH:])


def _inproj(x, gain, wqk, wvt, wc, qn, kn, seg, *, tm):
    s = x.shape[0]
    const = lambda i: (0, 0)
    return pl.pallas_call(
        _inproj_kernel,
        grid=(s // tm,),
        in_specs=[
            pl.BlockSpec((tm, D_MODEL), lambda i: (i, 0)),
            pl.BlockSpec((1, D_MODEL), const),
            pl.BlockSpec((D_MODEL, 2 * ATTN_WIDTH), const),
            pl.BlockSpec((ATTN_WIDTH, D_MODEL), const),
            pl.BlockSpec((D_MODEL, 2 * CONV_WIDTH), const),
            pl.BlockSpec((1, ATTN_WIDTH), const),
            pl.BlockSpec((1, ATTN_WIDTH), const),
            pl.BlockSpec((ATTN_WIDTH, ATTN_WIDTH), const),
        ],
        out_specs=[
            pl.BlockSpec((tm, ATTN_WIDTH), lambda i: (i, 0)),
            pl.BlockSpec((tm, ATTN_WIDTH), lambda i: (i, 0)),
            pl.BlockSpec((N_ATTN_HEADS * V_ROWS, tm), lambda i: (0, i)),
            pl.BlockSpec((tm, CONV_WIDTH), lambda i: (i, 0)),
        ],
        out_shape=[
            jax.ShapeDtypeStruct((s, ATTN_WIDTH), BF16),
            jax.ShapeDtypeStruct((s, ATTN_WIDTH), BF16),
            jax.ShapeDtypeStruct((N_ATTN_HEADS * V_ROWS, s), BF16),
            jax.ShapeDtypeStruct((s, CONV_WIDTH), F32),
        ],
        compiler_params=pltpu.CompilerParams(
            dimension_semantics=("arbitrary",), vmem_limit_bytes=VMEM_LIMIT_BYTES),
        name="inproj",
    )(x, gain, wqk, wvt, wc, qn, kn, seg)


N_POS_SPLIT = 3
POS_HI_STEP = 16
BF16_EXACT_INT = 256
EXP_STRIP = 64


def _bf16_pieces(value, n):
    pieces, rest = [], np.float64(value)
    for _ in range(n):
        piece = np.float64(np.asarray(rest, dtype=BF16).astype(np.float64))
        pieces.append(piece)
        rest = rest - piece
    return pieces


def _alibi_tables(tq, tk):
    assert tk // POS_HI_STEP <= BF16_EXACT_INT and tq <= BF16_EXACT_INT * BF16_EXACT_INT
    pos = np.arange(tk)
    qidx = np.arange(tq)
    feat = np.zeros((N_ATTN_HEADS, tk, V_HEAD_DIM), np.float64)
    qpos = np.zeros((N_ATTN_HEADS, V_HEAD_DIM, tq), np.float64)
    n_lin = 2 * N_POS_SPLIT
    for h in range(N_ATTN_HEADS):
        c = 2.0 ** (-8.0 * (h + 1) / N_ATTN_HEADS) * LOG2E
        for s, piece in enumerate(_bf16_pieces(c, N_POS_SPLIT)):
            feat[h, :, 2 * s] = pos // POS_HI_STEP
            feat[h, :, 2 * s + 1] = pos % POS_HI_STEP
            qpos[h, 2 * s, :] = -POS_HI_STEP * piece
            qpos[h, 2 * s + 1, :] = -piece
            feat[h, :, n_lin + 2 * s] = piece
            feat[h, :, n_lin + 2 * s + 1] = piece
            qpos[h, n_lin + 2 * s, :] = qidx % BF16_EXACT_INT
            qpos[h, n_lin + 2 * s + 1, :] = qidx - qidx % BF16_EXACT_INT
    return jnp.asarray(feat, BF16), jnp.asarray(qpos, BF16)


def _attn_kernel(c_ref, q_ref, k_ref, vt_ref, feat_ref, qpos_ref,
                 lq1_ref, lk1_ref, lq2_ref, lk2_ref, hn_ref,
                 o_ref, qz_ref, acc1_ref, acc2_ref, s_even_ref, s_odd_ref, p_even_ref, p_odd_ref, rel_ref,
                 *, tq, tk, seq, lam_init):
    head = pl.program_id(0)
    qi = pl.program_id(1)
    c = c_ref[head]
    q0 = qi * tq

    qt = q_ref[...].astype(F32).T.astype(BF16)
    row = lax.broadcasted_iota(jnp.int32, qt.shape, 0)
    zero = jnp.zeros_like(qt)
    q1t = jnp.where(row < DIFF_HEAD_DIM, qt, zero)
    q2t = jnp.where(row >= DIFF_HEAD_DIM, qt, zero)
    pos_after = qpos_ref[...]
    for side, pos in ((0, -pos_after), (1, pos_after)):
        qz_ref[side, 0:V_HEAD_DIM, 0:tq] = q1t
        qz_ref[side, 0:V_HEAD_DIM, tq:2 * tq] = q2t
        qz_ref[side, V_HEAD_DIM:, 0:tq] = pos
        qz_ref[side, V_HEAD_DIM:, tq:2 * tq] = pos

    acc1_ref[...] = jnp.zeros_like(acc1_ref)
    acc2_ref[...] = jnp.zeros_like(acc2_ref)

    n_chunks = seq // tk
    assert n_chunks % 2 == 0 and tk % tq == 0
    diag = lax.div(q0, tk)
    s_bufs = (s_even_ref, s_odd_ref)
    p_bufs = (p_even_ref, p_odd_ref)

    def chunk_of(i):
        after = (i > diag).astype(jnp.int32)
        return i - 1 + after, after

    def key_rows(j):
        return k_ref[pl.ds(pl.multiple_of(j * tk, tk), tk), :]

    def scores(i, s_ref):
        j, after = chunk_of(i)
        s = _dot(jnp.concatenate([key_rows(j), feat_ref[...]], axis=1), qz_ref[after])
        s_ref[...] = s
        return jnp.max(s, axis=0, keepdims=True)

    @pl.when((head == 0) & (qi == 0))
    def _():
        rel = (lax.broadcasted_iota(jnp.int32, (tk, tq), 0)
               - lax.broadcasted_iota(jnp.int32, (tk, tq), 1)).astype(F32)
        rel_ref[...] = jnp.abs(rel) if tq == tk else rel

    def scores_diag(s_ref):
        s = _dot(key_rows(diag), jnp.concatenate([q1t, q2t], axis=1))
        if tq == tk:
            dist = rel_ref[...]
        else:
            dist = jnp.abs(rel_ref[...] + (diag * tk - q0).astype(F32))
        bias = c * dist
        s = s - jnp.concatenate([bias, bias], axis=1)
        s_ref[...] = s
        return jnp.max(s, axis=0, keepdims=True)

    def chunk_offset(i):
        j, _ = chunk_of(i)
        return c * jnp.abs(j * tk - q0).astype(F32)

    def exps(s_ref, p_ref, cmax, off, m):
        m_new = jnp.maximum(m, cmax - off)
        shift = m_new + off
        for r in range(0, tk, EXP_STRIP):
            p_ref[r:r + EXP_STRIP, :] = jnp.exp2((s_ref[r:r + EXP_STRIP, :] - shift).astype(BF16))
        return m_new, jnp.exp2(m - m_new)

    def accumulate(p_ref, alpha, j):
        vtc = vt_ref[:, pl.ds(pl.multiple_of(j * tk, tk), tk)]
        acc1_ref[...] = alpha[:, :tq] * acc1_ref[...] + _dot(vtc, p_ref[:, :tq])
        acc2_ref[...] = alpha[:, tq:] * acc2_ref[...] + _dot(vtc, p_ref[:, tq:])

    def accumulate_visit(p_ref, alpha, i):
        accumulate(p_ref, alpha, jnp.where(i == 0, diag, chunk_of(i)[0]))

    cmax_diag = scores_diag(s_bufs[0])
    cmax = scores(1, s_bufs[1])
    m = jnp.minimum(cmax, NEG_BIG)
    m, alpha = exps(s_bufs[0], p_bufs[0], cmax_diag, 0.0, m)

    def body(t, carry):
        m, cmax, alpha = carry
        for parity in range(2):
            i = 2 * t + parity
            cmax_next = scores(i + 2, s_bufs[parity])
            m, alpha_next = exps(s_bufs[1 - parity], p_bufs[1 - parity], cmax, chunk_offset(i + 1), m)
            accumulate_visit(p_bufs[parity], alpha, i)
            cmax, alpha = cmax_next, alpha_next
        return m, cmax, alpha

    m, cmax, alpha = lax.fori_loop(0, (n_chunks - 2) // 2, body, (m, cmax, alpha))
    last = n_chunks - 1
    m, alpha_last = exps(s_bufs[1], p_bufs[1], cmax, chunk_offset(last), m)
    accumulate_visit(p_bufs[0], alpha, last - 1)
    accumulate_visit(p_bufs[1], alpha_last, last)

    lam = (jnp.exp(jnp.sum(lq1_ref[...] * lk1_ref[...], axis=-1, keepdims=True))
           - jnp.exp(jnp.sum(lq2_ref[...] * lk2_ref[...], axis=-1, keepdims=True))
           + lam_init)
    o_t = (acc1_ref[0:V_HEAD_DIM, :] / acc1_ref[V_HEAD_DIM:V_HEAD_DIM + 1, :]
           - lam * (acc2_ref[0:V_HEAD_DIM, :] / acc2_ref[V_HEAD_DIM:V_HEAD_DIM + 1, :]))
    o = o_t.T
    o_ref[...] = (_rmsnorm_rows(o, hn_ref[...]) * (1.0 - lam_init)).astype(o_ref.dtype)


def _attention(q, k, vt, lq1, lk1, lq2, lk2, hn, *, tq, tk, lam_init):
    s = q.shape[0]
    c = jnp.asarray([2.0 ** (-8.0 * (h + 1) / N_ATTN_HEADS) * LOG2E for h in range(N_ATTN_HEADS)], F32)
    feat, qpos = _alibi_tables(tq, tk)
    lam_spec = pl.BlockSpec((1, DIFF_HEAD_DIM), lambda h, i: (0, 0))
    kern = functools.partial(_attn_kernel, tq=tq, tk=tk, seq=s, lam_init=lam_init)
    return pl.pallas_call(
        kern,
        grid=(N_ATTN_HEADS, s // tq),
        in_specs=[
            pl.BlockSpec(memory_space=pltpu.SMEM),
            pl.BlockSpec((tq, V_HEAD_DIM), lambda h, i: (i, h)),
            pl.BlockSpec((s, V_HEAD_DIM), lambda h, i: (0, h)),
            pl.BlockSpec((V_ROWS, s), lambda h, i: (h, 0)),
            pl.BlockSpec((None, tk, V_HEAD_DIM), lambda h, i: (h, 0, 0)),
            pl.BlockSpec((None, V_HEAD_DIM, tq), lambda h, i: (h, 0, 0)),
            lam_spec, lam_spec, lam_spec, lam_spec,
            pl.BlockSpec((1, V_HEAD_DIM), lambda h, i: (0, 0)),
        ],
        out_specs=pl.BlockSpec((tq, V_HEAD_DIM), lambda h, i: (i, h)),
        out_shape=jax.ShapeDtypeStruct((s, ATTN_WIDTH), BF16),
        scratch_shapes=[pltpu.VMEM((2, 2 * V_HEAD_DIM, 2 * tq), BF16),
                        pltpu.VMEM((V_ROWS, tq), F32), pltpu.VMEM((V_ROWS, tq), F32),
                        pltpu.VMEM((tk, 2 * tq), F32), pltpu.VMEM((tk, 2 * tq), F32),
                        pltpu.VMEM((tk, 2 * tq), BF16), pltpu.VMEM((tk, 2 * tq), BF16),
                        pltpu.VMEM((tk, tq), F32)],
        compiler_params=pltpu.CompilerParams(
            dimension_semantics=("arbitrary", "arbitrary"), vmem_limit_bytes=VMEM_LIMIT_BYTES),
        name="diff_attention",
    )(c, q, k, vt, feat, qpos, lq1, lk1, lq2, lk2, hn)


def _conv_kernel(prev_ref, main_ref, next_ref, w_ref, b_ref, g_ref, beta_ref, o_ref, buf_ref,
                 shift_ref, *, ts, rows):
    i = pl.program_id(0)
    last = pl.num_programs(0) - 1
    buf_ref[0:CONV_HALO, :] = jnp.where(i > 0, prev_ref[...], 0.0)
    buf_ref[CONV_HALO:CONV_HALO + ts, :] = main_ref[...]
    buf_ref[CONV_HALO + ts:2 * CONV_HALO + ts, :] = jnp.where(i < last, next_ref[...], 0.0)
    n_shift = shift_ref.shape[1]
    for phase in range(1, SUBLANES):
        shift_ref[phase - 1] = buf_ref[phase:phase + n_shift, :]
    first_tap = CONV_HALO - CONV_KERNEL // 2
    for r0 in range(0, ts, rows):
        acc = jnp.broadcast_to(b_ref[...], (rows // SUBLANES, SUBLANES, CONV_WIDTH))
        for t in range(CONV_KERNEL):
            phase = (first_tap + t) % SUBLANES
            start = r0 + (first_tap + t) - phase
            src = buf_ref if phase == 0 else shift_ref.at[phase - 1]
            window = src[start:start + rows, :].reshape(rows // SUBLANES, SUBLANES, CONV_WIDTH)
            acc = acc + w_ref[t][None] * window
        acc = acc.reshape(rows, CONV_WIDTH)
        mu = jnp.mean(acc, axis=-1, keepdims=True)
        xc = acc - mu
        y = xc * lax.rsqrt(jnp.mean(xc * xc, axis=-1, keepdims=True) + EPS)
        y = y * g_ref[...] + beta_ref[...]
        o_ref[r0:r0 + rows, :] = (y * jax.nn.sigmoid(y)).astype(o_ref.dtype)


def _conformer_conv(u, w, b, g, beta, *, ts, rows):
    s = u.shape[0]
    nh = ts // CONV_HALO
    n_halo_blocks = s // CONV_HALO
    const = lambda i: (0, 0)
    kern = functools.partial(_conv_kernel, ts=ts, rows=rows)
    return pl.pallas_call(
        kern,
        grid=(s // ts,),
        in_specs=[
            pl.BlockSpec((CONV_HALO, CONV_WIDTH), lambda i: (jnp.maximum(i * nh - 1, 0), 0)),
            pl.BlockSpec((ts, CONV_WIDTH), lambda i: (i, 0)),
            pl.BlockSpec((CONV_HALO, CONV_WIDTH),
                         lambda i: (jnp.minimum((i + 1) * nh, n_halo_blocks - 1), 0)),
            pl.BlockSpec((CONV_KERNEL, SUBLANES, CONV_WIDTH), lambda i: (0, 0, 0)),
            pl.BlockSpec((1, CONV_WIDTH), const),
            pl.BlockSpec((1, CONV_WIDTH), const),
            pl.BlockSpec((1, CONV_WIDTH), const),
        ],
        out_specs=pl.BlockSpec((ts, CONV_WIDTH), lambda i: (i, 0)),
        out_shape=jax.ShapeDtypeStruct((s, CONV_WIDTH), BF16),
        scratch_shapes=[pltpu.VMEM((ts + 2 * CONV_HALO, CONV_WIDTH), F32),
                        pltpu.VMEM((SUBLANES - 1, ts + 2 * CONV_HALO - SUBLANES, CONV_WIDTH), F32)],
        compiler_params=pltpu.CompilerParams(
            dimension_semantics=("arbitrary",), vmem_limit_bytes=VMEM_LIMIT_BYTES),
        name="conformer_conv",
    )(u, u, u, w, b, g, beta)


FFN_CHUNKS = ((0, 1024), (1024, 2048), (2048, FFN_HIDDEN))


def _tail_kernel(x_ref, a_ref, c_ref, p_ref, woa_ref, woc_ref, fg_ref, wg_ref, wu_ref, wd_ref,
                 pg_ref, wpg_ref, wpp_ref, o_ref, act_ref):
    x1 = x_ref[...] + _dot(a_ref[...], woa_ref[...]) + _dot(c_ref[...], woc_ref[...])
    h = _rmsnorm_rows(x1, fg_ref[...]).astype(BF16)
    for c0, c1 in FFN_CHUNKS:
        g = _dot(h, wg_ref[:, c0:c1])
        u = _dot(h, wu_ref[:, c0:c1])
        act_ref[:, c0:c1] = (g * jax.nn.sigmoid(g) * u).astype(BF16)
    x2 = x1 + _dot(act_ref[...], wd_ref[...])
    hp = _rmsnorm_rows(x2, pg_ref[...]).astype(BF16)
    gate = jax.nn.sigmoid(_dot(hp, wpg_ref[...]))
    o_ref[...] = x2 + gate * _dot(p_ref[...].astype(BF16), wpp_ref[...])


def _tail(x, a, c, p, woa, woc, fg, wg, wu, wd, pg, wpg, wpp, *, tm):
    s = x.shape[0]
    const = lambda i: (0, 0)

    def resident(shape):
        return pl.BlockSpec(shape, const, pipeline_mode=pl.Buffered(1))

    return pl.pallas_call(
        _tail_kernel,
        grid=(s // tm,),
        in_specs=[
            pl.BlockSpec((tm, D_MODEL), lambda i: (i, 0)),
            pl.BlockSpec((tm, ATTN_WIDTH), lambda i: (i, 0)),
            pl.BlockSpec((tm, CONV_WIDTH), lambda i: (i, 0)),
            pl.BlockSpec((tm, PLE_DIM), lambda i: (i, 0)),
            resident((ATTN_WIDTH, D_MODEL)),
            resident((CONV_WIDTH, D_MODEL)),
            resident((1, D_MODEL)),
            resident((D_MODEL, FFN_HIDDEN)),
            resident((D_MODEL, FFN_HIDDEN)),
            resident((FFN_HIDDEN, D_MODEL)),
            resident((1, D_MODEL)),
            resident((D_MODEL, D_MODEL)),
            resident((PLE_DIM, D_MODEL)),
        ],
        out_specs=pl.BlockSpec((tm, D_MODEL), lambda i: (i, 0)),
        out_shape=jax.ShapeDtypeStruct((s, D_MODEL), F32),
        scratch_shapes=[pltpu.VMEM((tm, FFN_HIDDEN), BF16)],
        compiler_params=pltpu.CompilerParams(
            dimension_semantics=("arbitrary",), vmem_limit_bytes=VMEM_LIMIT_BYTES),
        name="tail",
    )(x, a, c, p, woa, woc, fg, wg, wu, wd, pg, wpg, wpp)


def _layer(x, p, lam_init, attn_norm, w_in, q_norm, k_norm, lq1, lk1, lq2, lk2, head_norm,
           conv_w, conv_b, conv_ln_g, conv_ln_b, w_out, ffn_norm, w_gate, w_up, w_down,
           ple_norm, w_ple_gate, w_ple_proj):
    row = lambda v: v.reshape(1, -1).astype(F32)
    a = ATTN_WIDTH
    wqk = w_in[:, :2 * a].astype(BF16)
    wvt = w_in[:, 2 * a:3 * a].T.astype(BF16)
    wc = w_in[:, 3 * a:].astype(BF16)
    n_seg = ATTN_WIDTH // DIFF_HEAD_DIM
    seg = jnp.asarray(np.kron(np.eye(n_seg), np.full((DIFF_HEAD_DIM, DIFF_HEAD_DIM),
                                                      1.0 / DIFF_HEAD_DIM)), BF16)
    qn = row(jnp.tile(q_norm, n_seg))
    kn = row(jnp.tile(k_norm, n_seg))
    q, k, vt, u = _inproj(x, row(attn_norm), wqk, wvt, wc, qn, kn, seg, tm=2 * ROW_TILE)

    attn = _attention(q, k, vt, row(lq1), row(lk1), row(lq2), row(lk2), row(head_norm),
                      tq=ATTN_Q_TILE, tk=ATTN_K_CHUNK, lam_init=lam_init)
    conv_taps = jnp.broadcast_to(conv_w.astype(F32)[:, None, :], (CONV_KERNEL, SUBLANES, CONV_WIDTH))
    conv = _conformer_conv(u, conv_taps, row(conv_b), row(conv_ln_g), row(conv_ln_b),
                           ts=2 * ROW_TILE, rows=CONV_ROW_BLOCK)
    return _tail(x, attn, conv, p, w_out[:a].astype(BF16), w_out[a:].astype(BF16), row(ffn_norm),
                 w_gate.astype(BF16), w_up.astype(BF16), w_down.astype(BF16), row(ple_norm),
                 w_ple_gate.astype(BF16), w_ple_proj.astype(BF16), tm=ROW_TILE)


def kernel(x, p, attn_norm, w_in, q_norm, k_norm, lambda_q1, lambda_k1, lambda_q2, lambda_k2,
           head_norm, conv_w, conv_b, conv_ln_g, conv_ln_b, w_out, ffn_norm, w_gate, w_up, w_down,
           ple_norm, w_ple_gate, w_ple_proj):
    depth = w_in.shape[0]
    batch = x.shape[0]
    outs = []
    for b in range(batch):
        xb = x[b]
        for i in range(depth):
            lam_init = 0.8 - 0.6 * math.exp(-0.3 * i)
            xb = _layer(xb, p[i, b], lam_init, attn_norm[i], w_in[i], q_norm[i], k_norm[i],
                        lambda_q1[i], lambda_k1[i], lambda_q2[i], lambda_k2[i], head_norm[i],
                        conv_w[i], conv_b[i], conv_ln_g[i], conv_ln_b[i], w_out[i], ffn_norm[i],
                        w_gate[i], w_up[i], w_down[i], ple_norm[i], w_ple_gate[i], w_ple_proj[i])
        outs.append(xb)
    return jnp.stack(outs, axis=0)
```

```python
import functools
import math

import numpy as np
import jax
import jax.numpy as jnp
from jax import lax
from jax.experimental import pallas as pl
from jax.experimental.pallas import tpu as pltpu

D_MODEL = 1024
ATTN_WIDTH = 512
CONV_WIDTH = 512
DIFF_HEAD_DIM = 64
V_HEAD_DIM = 128
V_ROWS = V_HEAD_DIM + 16
N_ATTN_HEADS = 4
CONV_KERNEL = 31
CONV_HALO = 16
SUBLANES = 8
FFN_HIDDEN = 2816
PLE_DIM = 256
EPS = 1e-6

VMEM_LIMIT_BYTES = 56 * 1024 * 1024

ROW_TILE = 512
CONV_ROW_BLOCK = 32
ATTN_Q_TILE = 1024
ATTN_K_CHUNK = 1024

F32 = jnp.float32
BF16 = jnp.bfloat16
NEG_BIG = -1e30
LOG2E = math.log2(math.e)


def _rmsnorm_rows(x, gain):
    ms = jnp.mean(x * x, axis=-1, keepdims=True)
    return x * lax.rsqrt(ms + EPS) * gain


def _dot(a, b):
    return jnp.dot(a, b, preferred_element_type=F32)


def _dot_nt(a, b):
    return lax.dot_general(a, b, (((1,), (1,)), ((), ())), preferred_element_type=F32)


def _inproj_kernel(x_ref, g_ref, wqk_ref, wvt_ref, wc_ref, qn_ref, kn_ref, seg_ref,
                   q_ref, k_ref, vt_ref, u_ref):
    h = _rmsnorm_rows(x_ref[...], g_ref[...]).astype(BF16)
    qk = _dot(h, wqk_ref[...])

    def seg_norm(y, gain):
        ms = _dot((y * y).astype(BF16), seg_ref[...])
        return y * lax.rsqrt(ms + EPS) * gain

    q = seg_norm(qk[:, :ATTN_WIDTH], qn_ref[...]) * (DIFF_HEAD_DIM ** -0.5 * LOG2E)
    k = seg_norm(qk[:, ATTN_WIDTH:], kn_ref[...])
    q_ref[...] = q.astype(BF16)
    k_ref[...] = k.astype(BF16)
    vt = _dot_nt(wvt_ref[...], h).astype(BF16)
    ones = jnp.ones((V_ROWS - V_HEAD_DIM, vt.shape[1]), BF16)
    for hd in range(N_ATTN_HEADS):
        vt_ref[hd * V_ROWS:hd * V_ROWS + V_HEAD_DIM, :] = vt[hd * V_HEAD_DIM:(hd + 1) * V_HEAD_DIM]
        vt_ref[hd * V_ROWS + V_HEAD_DIM:(hd + 1) * V_ROWS, :] = ones
    c = _dot(h, wc_ref[...])
    u_ref[...] = c[:, :CONV_WIDTH] * jax.nn.sigmoid(c[:, CONV_WIDTH:])


def _inproj(x, gain, wqk, wvt, wc, qn, kn, seg, *, tm):
    s = x.shape[0]
    const = lambda i: (0, 0)
    return pl.pallas_call(
        _inproj_kernel,
        grid=(s // tm,),
        in_specs=[
            pl.BlockSpec((tm, D_MODEL), lambda i: (i, 0)),
            pl.BlockSpec((1, D_MODEL), const),
            pl.BlockSpec((D_MODEL, 2 * ATTN_WIDTH), const),
            pl.BlockSpec((ATTN_WIDTH, D_MODEL), const),
            pl.BlockSpec((D_MODEL, 2 * CONV_WIDTH), const),
            pl.BlockSpec((1, ATTN_WIDTH), const),
            pl.BlockSpec((1, ATTN_WIDTH), const),
            pl.BlockSpec((ATTN_WIDTH, ATTN_WIDTH), const),
        ],
        out_specs=[
            pl.BlockSpec((tm, ATTN_WIDTH), lambda i: (i, 0)),
            pl.BlockSpec((tm, ATTN_WIDTH), lambda i: (i, 0)),
            pl.BlockSpec((N_ATTN_HEADS * V_ROWS, tm), lambda i: (0, i)),
            pl.BlockSpec((tm, CONV_WIDTH), lambda i: (i, 0)),
        ],
        out_shape=[
            jax.ShapeDtypeStruct((s, ATTN_WIDTH), BF16),
            jax.ShapeDtypeStruct((s, ATTN_WIDTH), BF16),
            jax.ShapeDtypeStruct((N_ATTN_HEADS * V_ROWS, s), BF16),
            jax.ShapeDtypeStruct((s, CONV_WIDTH), F32),
        ],
        compiler_params=pltpu.CompilerParams(
            dimension_semantics=("arbitrary",), vmem_limit_bytes=VMEM_LIMIT_BYTES),
        name="inproj",
    )(x, gain, wqk, wvt, wc, qn, kn, seg)


N_POS_SPLIT = 3
POS_HI_STEP = 16
BF16_EXACT_INT = 256
EXP_STRIP = 64


def _bf16_pieces(value, n):
    pieces, rest = [], np.float64(value)
    for _ in range(n):
        piece = np.float64(np.asarray(rest, dtype=BF16).astype(np.float64))
        pieces.append(piece)
        rest = rest - piece
    return pieces


def _alibi_tables(tq, tk):
    assert tk // POS_HI_STEP <= BF16_EXACT_INT and tq <= BF16_EXACT_INT * BF16_EXACT_INT
    pos = np.arange(tk)
    qidx = np.arange(tq)
    feat = np.zeros((N_ATTN_HEADS, tk, V_HEAD_DIM), np.float64)
    qpos = np.zeros((N_ATTN_HEADS, V_HEAD_DIM, tq), np.float64)
    n_lin = 2 * N_POS_SPLIT
    for h in range(N_ATTN_HEADS):
        c = 2.0 ** (-8.0 * (h + 1) / N_ATTN_HEADS) * LOG2E
        for s, piece in enumerate(_bf16_pieces(c, N_POS_SPLIT)):
            feat[h, :, 2 * s] = pos // POS_HI_STEP
            feat[h, :, 2 * s + 1] = pos % POS_HI_STEP
            qpos[h, 2 * s, :] = -POS_HI_STEP * piece
            qpos[h, 2 * s + 1, :] = -piece
            feat[h, :, n_lin + 2 * s] = piece
            feat[h, :, n_lin + 2 * s + 1] = piece
            qpos[h, n_lin + 2 * s, :] = qidx % BF16_EXACT_INT
            qpos[h, n_lin + 2 * s + 1, :] = qidx - qidx % BF16_EXACT_INT
    both_maps = np.concatenate([qpos, qpos], axis=-1)
    return jnp.asarray(feat, BF16), jnp.asarray(np.stack([-both_maps, both_maps], axis=1), BF16)


def _attn_kernel(c_ref, q_ref, k_ref, vt_ref, feat_ref, qpos_ref,
                 lq1_ref, lk1_ref, lq2_ref, lk2_ref, hn_ref,
                 o_ref, qz_ref, acc1_ref, acc2_ref, s_even_ref, s_odd_ref, p_even_ref, p_odd_ref, rel_ref,
                 *, tq, tk, seq, lam_init):
    head = pl.program_id(0)
    qi = pl.program_id(1)
    c = c_ref[head]
    q0 = qi * tq

    qt = q_ref[...].astype(F32).T.astype(BF16)
    row = lax.broadcasted_iota(jnp.int32, qt.shape, 0)
    zero = jnp.zeros_like(qt)
    qz_ref[:, 0:tq] = jnp.where(row < DIFF_HEAD_DIM, qt, zero)
    qz_ref[:, tq:2 * tq] = jnp.where(row >= DIFF_HEAD_DIM, qt, zero)

    acc1_ref[...] = jnp.zeros_like(acc1_ref)
    acc2_ref[...] = jnp.zeros_like(acc2_ref)

    n_chunks = seq // tk
    assert n_chunks % 2 == 0 and tk % tq == 0
    diag = lax.div(q0, tk)
    s_bufs = (s_even_ref, s_odd_ref)
    p_bufs = (p_even_ref, p_odd_ref)

    def chunk_of(i):
        after = (i > diag).astype(jnp.int32)
        return i - 1 + after, after

    def key_rows(j):
        return k_ref[pl.ds(pl.multiple_of(j * tk, tk), tk), :]

    def scores(i, s_ref):
        j, after = chunk_of(i)
        rhs = jnp.concatenate([qz_ref[...], qpos_ref[after]], axis=0)
        s = _dot(jnp.concatenate([key_rows(j), feat_ref[...]], axis=1), rhs)
        s_ref[...] = s
        return jnp.max(s, axis=0, keepdims=True)

    @pl.when((head == 0) & (qi == 0))
    def _():
        rel = (lax.broadcasted_iota(jnp.int32, (tk, tq), 0)
               - lax.broadcasted_iota(jnp.int32, (tk, tq), 1)).astype(F32)
        rel_ref[...] = jnp.abs(rel) if tq == tk else rel

    def scores_diag(s_ref):
        s = _dot(key_rows(diag), qz_ref[...])
        if tq == tk:
            dist = rel_ref[...]
        else:
            dist = jnp.abs(rel_ref[...] + (diag * tk - q0).astype(F32))
        bias = c * dist
        s = s - jnp.concatenate([bias, bias], axis=1)
        s_ref[...] = s
        return jnp.max(s, axis=0, keepdims=True)

    def chunk_offset(i):
        j, _ = chunk_of(i)
        return c * jnp.abs(j * tk - q0).astype(F32)

    def exps(s_ref, p_ref, cmax, off, m):
        m_new = jnp.maximum(m, cmax - off)
        shift = m_new + off
        for r in range(0, tk, EXP_STRIP):
            p_ref[r:r + EXP_STRIP, :] = jnp.exp2((s_ref[r:r + EXP_STRIP, :] - shift).astype(BF16))
        return m_new, jnp.exp2(m - m_new)

    def accumulate(p_ref, alpha, j):
        vtc = vt_ref[:, pl.ds(pl.multiple_of(j * tk, tk), tk)]
        acc1_ref[...] = alpha[:, :tq] * acc1_ref[...] + _dot(vtc, p_ref[:, :tq])
        acc2_ref[...] = alpha[:, tq:] * acc2_ref[...] + _dot(vtc, p_ref[:, tq:])

    def accumulate_visit(p_ref, alpha, i):
        accumulate(p_ref, alpha, jnp.where(i == 0, diag, chunk_of(i)[0]))

    cmax_diag = scores_diag(s_bufs[0])
    cmax = scores(1, s_bufs[1])
    m = jnp.minimum(cmax, NEG_BIG)
    m, alpha = exps(s_bufs[0], p_bufs[0], cmax_diag, 0.0, m)

    def body(t, carry):
        m, cmax, alpha = carry
        for parity in range(2):
            i = 2 * t + parity
            cmax_next = scores(i + 2, s_bufs[parity])
            m, alpha_next = exps(s_bufs[1 - parity], p_bufs[1 - parity], cmax, chunk_offset(i + 1), m)
            accumulate_visit(p_bufs[parity], alpha, i)
            cmax, alpha = cmax_next, alpha_next
        return m, cmax, alpha

    m, cmax, alpha = lax.fori_loop(0, (n_chunks - 2) // 2, body, (m, cmax, alpha))
    last = n_chunks - 1
    m, alpha_last = exps(s_bufs[1], p_bufs[1], cmax, chunk_offset(last), m)
    accumulate_visit(p_bufs[0], alpha, last - 1)
    accumulate_visit(p_bufs[1], alpha_last, last)

    lam = (jnp.exp(jnp.sum(lq1_ref[...] * lk1_ref[...], axis=-1, keepdims=True))
           - jnp.exp(jnp.sum(lq2_ref[...] * lk2_ref[...], axis=-1, keepdims=True))
           + lam_init)
    o_t = (acc1_ref[0:V_HEAD_DIM, :] / acc1_ref[V_HEAD_DIM:V_HEAD_DIM + 1, :]
           - lam * (acc2_ref[0:V_HEAD_DIM, :] / acc2_ref[V_HEAD_DIM:V_HEAD_DIM + 1, :]))
    o = o_t.T
    o_ref[...] = (_rmsnorm_rows(o, hn_ref[...]) * (1.0 - lam_init)).astype(o_ref.dtype)


def _attention(q, k, vt, lq1, lk1, lq2, lk2, hn, *, tq, tk, lam_init):
    s = q.shape[0]
    c = jnp.asarray([2.0 ** (-8.0 * (h + 1) / N_ATTN_HEADS) * LOG2E for h in range(N_ATTN_HEADS)], F32)
    feat, qpos = _alibi_tables(tq, tk)
    lam_spec = pl.BlockSpec((1, DIFF_HEAD_DIM), lambda h, i: (0, 0))
    kern = functools.partial(_attn_kernel, tq=tq, tk=tk, seq=s, lam_init=lam_init)
    return pl.pallas_call(
        kern,
        grid=(N_ATTN_HEADS, s // tq),
        in_specs=[
            pl.BlockSpec(memory_space=pltpu.SMEM),
            pl.BlockSpec((tq, V_HEAD_DIM), lambda h, i: (i, h)),
            pl.BlockSpec((s, V_HEAD_DIM), lambda h, i: (0, h)),
            pl.BlockSpec((V_ROWS, s), lambda h, i: (h, 0)),
            pl.BlockSpec((None, tk, V_HEAD_DIM), lambda h, i: (h, 0, 0)),
            pl.BlockSpec((None, 2, V_HEAD_DIM, 2 * tq), lambda h, i: (h, 0, 0, 0)),
            lam_spec, lam_spec, lam_spec, lam_spec,
            pl.BlockSpec((1, V_HEAD_DIM), lambda h, i: (0, 0)),
        ],
        out_specs=pl.BlockSpec((tq, V_HEAD_DIM), lambda h, i: (i, h)),
        out_shape=jax.ShapeDtypeStruct((s, ATTN_WIDTH), BF16),
        scratch_shapes=[pltpu.VMEM((V_HEAD_DIM, 2 * tq), BF16),
                        pltpu.VMEM((V_ROWS, tq), F32), pltpu.VMEM((V_ROWS, tq), F32),
                        pltpu.VMEM((tk, 2 * tq), F32), pltpu.VMEM((tk, 2 * tq), F32),
                        pltpu.VMEM((tk, 2 * tq), BF16), pltpu.VMEM((tk, 2 * tq), BF16),
                        pltpu.VMEM((tk, tq), F32)],
        compiler_params=pltpu.CompilerParams(
            dimension_semantics=("arbitrary", "arbitrary"), vmem_limit_bytes=VMEM_LIMIT_BYTES),
        name="diff_attention",
    )(c, q, k, vt, feat, qpos, lq1, lk1, lq2, lk2, hn)


def _conv_kernel(prev_ref, main_ref, next_ref, w_ref, b_ref, g_ref, beta_ref, o_ref, buf_ref,
                 shift_ref, *, ts, rows):
    i = pl.program_id(0)
    last = pl.num_programs(0) - 1
    buf_ref[0:CONV_HALO, :] = jnp.where(i > 0, prev_ref[...], 0.0)
    buf_ref[CONV_HALO:CONV_HALO + ts, :] = main_ref[...]
    buf_ref[CONV_HALO + ts:2 * CONV_HALO + ts, :] = jnp.where(i < last, next_ref[...], 0.0)
    n_shift = shift_ref.shape[1]
    for phase in range(1, SUBLANES):
        shift_ref[phase - 1] = buf_ref[phase:phase + n_shift, :]
    first_tap = CONV_HALO - CONV_KERNEL // 2
    for r0 in range(0, ts, rows):
        acc = jnp.broadcast_to(b_ref[...], (rows // SUBLANES, SUBLANES, CONV_WIDTH))
        for t in range(CONV_KERNEL):
            phase = (first_tap + t) % SUBLANES
            start = r0 + (first_tap + t) - phase
            src = buf_ref if phase == 0 else shift_ref.at[phase - 1]
            window = src[start:start + rows, :].reshape(rows // SUBLANES, SUBLANES, CONV_WIDTH)
            acc = acc + w_ref[t][None] * window
        acc = acc.reshape(rows, CONV_WIDTH)
        mu = jnp.mean(acc, axis=-1, keepdims=True)
        xc = acc - mu
        y = xc * lax.rsqrt(jnp.mean(xc * xc, axis=-1, keepdims=True) + EPS)
        y = y * g_ref[...] + beta_ref[...]
        o_ref[r0:r0 + rows, :] = (y * jax.nn.sigmoid(y)).astype(o_ref.dtype)


def _conformer_conv(u, w, b, g, beta, *, ts, rows):
    s = u.shape[0]
    nh = ts // CONV_HALO
    n_halo_blocks = s // CONV_HALO
    const = lambda i: (0, 0)
    kern = functools.partial(_conv_kernel, ts=ts, rows=rows)
    return pl.pallas_call(
        kern,
        grid=(s // ts,),
        in_specs=[
            pl.BlockSpec((CONV_HALO, CONV_WIDTH), lambda i: (jnp.maximum(i * nh - 1, 0), 0)),
            pl.BlockSpec((ts, CONV_WIDTH), lambda i: (i, 0)),
            pl.BlockSpec((CONV_HALO, CONV_WIDTH),
                         lambda i: (jnp.minimum((i + 1) * nh, n_halo_blocks - 1), 0)),
            pl.BlockSpec((CONV_KERNEL, SUBLANES, CONV_WIDTH), lambda i: (0, 0, 0)),
            pl.BlockSpec((1, CONV_WIDTH), const),
            pl.BlockSpec((1, CONV_WIDTH), const),
            pl.BlockSpec((1, CONV_WIDTH), const),
        ],
        out_specs=pl.BlockSpec((ts, CONV_WIDTH), lambda i: (i, 0)),
        out_shape=jax.ShapeDtypeStruct((s, CONV_WIDTH), BF16),
        scratch_shapes=[pltpu.VMEM((ts + 2 * CONV_HALO, CONV_WIDTH), F32),
                        pltpu.VMEM((SUBLANES - 1, ts + 2 * CONV_HALO - SUBLANES, CONV_WIDTH), F32)],
        compiler_params=pltpu.CompilerParams(
            dimension_semantics=("arbitrary",), vmem_limit_bytes=VMEM_LIMIT_BYTES),
        name="conformer_conv",
    )(u, u, u, w, b, g, beta)


FFN_CHUNKS = ((0, 1024), (1024, 2048), (2048, FFN_HIDDEN))


def _tail_kernel(x_ref, a_ref, c_ref, p_ref, woa_ref, woc_ref, fg_ref, wg_ref, wu_ref, wd_ref,
                 pg_ref, wpg_ref, wpp_ref, o_ref, act_ref):
    x1 = x_ref[...] + _dot(a_ref[...], woa_ref[...]) + _dot(c_ref[...], woc_ref[...])
    h = _rmsnorm_rows(x1, fg_ref[...]).astype(BF16)
    for c0, c1 in FFN_CHUNKS:
        g = _dot(h, wg_ref[:, c0:c1])
        u = _dot(h, wu_ref[:, c0:c1])
        act_ref[:, c0:c1] = (g * jax.nn.sigmoid(g) * u).astype(BF16)
    x2 = x1 + _dot(act_ref[...], wd_ref[...])
    hp = _rmsnorm_rows(x2, pg_ref[...]).astype(BF16)
    gate = jax.nn.sigmoid(_dot(hp, wpg_ref[...]))
    o_ref[...] = x2 + gate * _dot(p_ref[...].astype(BF16), wpp_ref[...])


def _tail(x, a, c, p, woa, woc, fg, wg, wu, wd, pg, wpg, wpp, *, tm):
    s = x.shape[0]
    const = lambda i: (0, 0)

    def resident(shape):
        return pl.BlockSpec(shape, const, pipeline_mode=pl.Buffered(1))

    return pl.pallas_call(
        _tail_kernel,
        grid=(s // tm,),
        in_specs=[
            pl.BlockSpec((tm, D_MODEL), lambda i: (i, 0)),
            pl.BlockSpec((tm, ATTN_WIDTH), lambda i: (i, 0)),
            pl.BlockSpec((tm, CONV_WIDTH), lambda i: (i, 0)),
            pl.BlockSpec((tm, PLE_DIM), lambda i: (i, 0)),
            resident((ATTN_WIDTH, D_MODEL)),
            resident((CONV_WIDTH, D_MODEL)),
            resident((1, D_MODEL)),
            resident((D_MODEL, FFN_HIDDEN)),
            resident((D_MODEL, FFN_HIDDEN)),
            resident((FFN_HIDDEN, D_MODEL)),
            resident((1, D_MODEL)),
            resident((D_MODEL, D_MODEL)),
            resident((PLE_DIM, D_MODEL)),
        ],
        out_specs=pl.BlockSpec((tm, D_MODEL), lambda i: (i, 0)),
        out_shape=jax.ShapeDtypeStruct((s, D_MODEL), F32),
        scratch_shapes=[pltpu.VMEM((tm, FFN_HIDDEN), BF16)],
        compiler_params=pltpu.CompilerParams(
            dimension_semantics=("arbitrary",), vmem_limit_bytes=VMEM_LIMIT_BYTES),
        name="tail",
    )(x, a, c, p, woa, woc, fg, wg, wu, wd, pg, wpg, wpp)


def _layer(x, p, lam_init, attn_norm, w_in, q_norm, k_norm, lq1, lk1, lq2, lk2, head_norm,
           conv_w, conv_b, conv_ln_g, conv_ln_b, w_out, ffn_norm, w_gate, w_up, w_down,
           ple_norm, w_ple_gate, w_ple_proj):
    row = lambda v: v.reshape(1, -1).astype(F32)
    a = ATTN_WIDTH
    wqk = w_in[:, :2 * a].astype(BF16)
    wvt = w_in[:, 2 * a:3 * a].T.astype(BF16)
    wc = w_in[:, 3 * a:].astype(BF16)
    n_seg = ATTN_WIDTH // DIFF_HEAD_DIM
    seg = jnp.asarray(np.kron(np.eye(n_seg), np.full((DIFF_HEAD_DIM, DIFF_HEAD_DIM),
                                                      1.0 / DIFF_HEAD_DIM)), BF16)
    qn = row(jnp.tile(q_norm, n_seg))
    kn = row(jnp.tile(k_norm, n_seg))
    q, k, vt, u = _inproj(x, row(attn_norm), wqk, wvt, wc, qn, kn, seg, tm=2 * ROW_TILE)

    attn = _attention(q, k, vt, row(lq1), row(lk1), row(lq2), row(lk2), row(head_norm),
                      tq=ATTN_Q_TILE, tk=ATTN_K_CHUNK, lam_init=lam_init)
    conv_taps = jnp.broadcast_to(conv_w.astype(F32)[:, None, :], (CONV_KERNEL, SUBLANES, CONV_WIDTH))
    conv = _conformer_conv(u, conv_taps, row(conv_b), row(conv_ln_g), row(conv_ln_b),
                           ts=2 * ROW_TILE, rows=CONV_ROW_BLOCK)
    return _tail(x, attn, conv, p, w_out[:a].astype(BF16), w_out[a:].astype(BF16), row(ffn_norm),
                 w_gate.astype(BF16), w_up.astype(BF16), w_down.astype(BF16), row(ple_norm),
                 w_ple_gate.astype(BF16), w_ple_proj.astype(BF16), tm=ROW_TILE)


def kernel(x, p, attn_norm, w_in, q_norm, k_norm, lambda_q1, lambda_k1, lambda_q2, lambda_k2,
           head_norm, conv_w, conv_b, conv_ln_g, conv_ln_b, w_out, ffn_norm, w_gate, w_up, w_down,
           ple_norm, w_ple_gate, w_ple_proj):
    depth = w_in.shape[0]
    batch = x.shape[0]
    outs = []
    for b in range(batch):
        xb = x[b]
        for i in range(depth):
            lam_init = 0.8 - 0.6 * math.exp(-0.3 * i)
            xb = _layer(xb, p[i, b], lam_init, attn_norm[i], w_in[i], q_norm[i], k_norm[i],
                        lambda_q1[i], lambda_k1[i], lambda_q2[i], lambda_k2[i], head_norm[i],
                        conv_w[i], conv_b[i], conv_ln_g[i], conv_ln_b[i], w_out[i], ffn_norm[i],
                        w_gate[i], w_up[i], w_down[i], ple_norm[i], w_ple_gate[i], w_ple_proj[i])
        outs.append(xb)
    return jnp.stack(outs, axis=0)
```

```python
import functools
import math

import numpy as np
import jax
import jax.numpy as jnp
from jax import lax
from jax.experimental import pallas as pl
from jax.experimental.pallas import tpu as pltpu

D_MODEL = 1024
ATTN_WIDTH = 512
CONV_WIDTH = 512
DIFF_HEAD_DIM = 64
V_HEAD_DIM = 128
V_ROWS = V_HEAD_DIM + 16
N_ATTN_HEADS = 4
CONV_KERNEL = 31
CONV_HALO = 16
SUBLANES = 8
FFN_HIDDEN = 2816
PLE_DIM = 256
EPS = 1e-6

VMEM_LIMIT_BYTES = 56 * 1024 * 1024

ROW_TILE = 512
CONV_ROW_BLOCK = 32
ATTN_Q_TILE = 1024
ATTN_K_CHUNK = 1024

F32 = jnp.float32
BF16 = jnp.bfloat16
NEG_BIG = -1e30
LOG2E = math.log2(math.e)


def _rmsnorm_rows(x, gain):
    ms = jnp.mean(x * x, axis=-1, keepdims=True)
    return x * lax.rsqrt(ms + EPS) * gain


def _dot(a, b):
    return jnp.dot(a, b, preferred_element_type=F32)


def _dot_nt(a, b):
    return lax.dot_general(a, b, (((1,), (1,)), ((), ())), preferred_element_type=F32)


def _inproj_kernel(x_ref, g_ref, wqk_ref, wvt_ref, wc_ref, qn_ref, kn_ref, seg_ref,
                   q_ref, k_ref, vt_ref, u_ref):
    h = _rmsnorm_rows(x_ref[...], g_ref[...]).astype(BF16)
    qk = _dot(h, wqk_ref[...])

    def seg_norm(y, gain):
        ms = _dot((y * y).astype(BF16), seg_ref[...])
        return y * lax.rsqrt(ms + EPS) * gain

    q = seg_norm(qk[:, :ATTN_WIDTH], qn_ref[...]) * (DIFF_HEAD_DIM ** -0.5 * LOG2E)
    k = seg_norm(qk[:, ATTN_WIDTH:], kn_ref[...])
    q_ref[...] = q.astype(BF16)
    k_ref[...] = k.astype(BF16)
    vt = _dot_nt(wvt_ref[...], h).astype(BF16)
    ones = jnp.ones((V_ROWS - V_HEAD_DIM, vt.shape[1]), BF16)
    for hd in range(N_ATTN_HEADS):
        vt_ref[hd * V_ROWS:hd * V_ROWS + V_HEAD_DIM, :] = vt[hd * V_HEAD_DIM:(hd + 1) * V_HEAD_DIM]
        vt_ref[hd * V_ROWS + V_HEAD_DIM:(hd + 1) * V_ROWS, :] = ones
    c = _dot(h, wc_ref[...])
    u_ref[...] = c[:, :CONV_WIDTH] * jax.nn.sigmoid(c[:, CONV_WIDTH:])


def _inproj(x, gain, wqk, wvt, wc, qn, kn, seg, *, tm):
    s = x.shape[0]
    const = lambda i: (0, 0)
    return pl.pallas_call(
        _inproj_kernel,
        grid=(s // tm,),
        in_specs=[
            pl.BlockSpec((tm, D_MODEL), lambda i: (i, 0)),
            pl.BlockSpec((1, D_MODEL), const),
            pl.BlockSpec((D_MODEL, 2 * ATTN_WIDTH), const),
            pl.BlockSpec((ATTN_WIDTH, D_MODEL), const),
            pl.BlockSpec((D_MODEL, 2 * CONV_WIDTH), const),
            pl.BlockSpec((1, ATTN_WIDTH), const),
            pl.BlockSpec((1, ATTN_WIDTH), const),
            pl.BlockSpec((ATTN_WIDTH, ATTN_WIDTH), const),
        ],
        out_specs=[
            pl.BlockSpec((tm, ATTN_WIDTH), lambda i: (i, 0)),
            pl.BlockSpec((tm, ATTN_WIDTH), lambda i: (i, 0)),
            pl.BlockSpec((N_ATTN_HEADS * V_ROWS, tm), lambda i: (0, i)),
            pl.BlockSpec((tm, CONV_WIDTH), lambda i: (i, 0)),
        ],
        out_shape=[
            jax.ShapeDtypeStruct((s, ATTN_WIDTH), BF16),
            jax.ShapeDtypeStruct((s, ATTN_WIDTH), BF16),
            jax.ShapeDtypeStruct((N_ATTN_HEADS * V_ROWS, s), BF16),
            jax.ShapeDtypeStruct((s, CONV_WIDTH), F32),
        ],
        compiler_params=pltpu.CompilerParams(
            dimension_semantics=("arbitrary",), vmem_limit_bytes=VMEM_LIMIT_BYTES),
        name="inproj",
    )(x, gain, wqk, wvt, wc, qn, kn, seg)


N_POS_SPLIT = 3
POS_HI_STEP = 16
BF16_EXACT_INT = 256
EXP_STRIP = 64


def _bf16_pieces(value, n):
    pieces, rest = [], np.float64(value)
    for _ in range(n):
        piece = np.float64(np.asarray(rest, dtype=BF16).astype(np.float64))
        pieces.append(piece)
        rest = rest - piece
    return pieces


def _alibi_tables(tq, tk):
    assert tk // POS_HI_STEP <= BF16_EXACT_INT and tq <= BF16_EXACT_INT * BF16_EXACT_INT
    pos = np.arange(tk)
    qidx = np.arange(tq)
    feat = np.zeros((N_ATTN_HEADS, tk, V_HEAD_DIM), np.float64)
    qpos = np.zeros((N_ATTN_HEADS, V_HEAD_DIM, tq), np.float64)
    n_lin = 2 * N_POS_SPLIT
    for h in range(N_ATTN_HEADS):
        c = 2.0 ** (-8.0 * (h + 1) / N_ATTN_HEADS) * LOG2E
        for s, piece in enumerate(_bf16_pieces(c, N_POS_SPLIT)):
            feat[h, :, 2 * s] = pos // POS_HI_STEP
            feat[h, :, 2 * s + 1] = pos % POS_HI_STEP
            qpos[h, 2 * s, :] = -POS_HI_STEP * piece
            qpos[h, 2 * s + 1, :] = -piece
            feat[h, :, n_lin + 2 * s] = piece
            feat[h, :, n_lin + 2 * s + 1] = piece
            qpos[h, n_lin + 2 * s, :] = qidx % BF16_EXACT_INT
            qpos[h, n_lin + 2 * s + 1, :] = qidx - qidx % BF16_EXACT_INT
    both_maps = np.concatenate([qpos, qpos], axis=-1)
    return jnp.asarray(feat, BF16), jnp.asarray(np.stack([-both_maps, both_maps], axis=1), BF16)


def _attn_kernel(c_ref, q_ref, k_ref, vt_ref, feat_ref, qpos_ref,
                 lq1_ref, lk1_ref, lq2_ref, lk2_ref, hn_ref,
                 o_ref, qz_ref, acc1_ref, acc2_ref, s_even_ref, s_odd_ref, p_even_ref, p_odd_ref, rel_ref,
                 *, tq, tk, seq, lam_init):
    head = pl.program_id(0)
    qi = pl.program_id(1)
    c = c_ref[head]
    q0 = qi * tq

    qt = q_ref[...].astype(F32).T.astype(BF16)
    row = lax.broadcasted_iota(jnp.int32, qt.shape, 0)
    zero = jnp.zeros_like(qt)
    qz_ref[:, 0:tq] = jnp.where(row < DIFF_HEAD_DIM, qt, zero)
    qz_ref[:, tq:2 * tq] = jnp.where(row >= DIFF_HEAD_DIM, qt, zero)

    acc1_ref[...] = jnp.zeros_like(acc1_ref)
    acc2_ref[...] = jnp.zeros_like(acc2_ref)

    n_chunks = seq // tk
    assert n_chunks % 2 == 0 and tk % tq == 0
    diag = lax.div(q0, tk)
    s_bufs = (s_even_ref, s_odd_ref)
    p_bufs = (p_even_ref, p_odd_ref)

    def chunk_of(i):
        after = (i > diag).astype(jnp.int32)
        return i - 1 + after, after

    def key_rows(j):
        return k_ref[pl.ds(pl.multiple_of(j * tk, tk), tk), :]

    def scores(i, s_ref):
        j, after = chunk_of(i)
        rhs = jnp.concatenate([qz_ref[...], qpos_ref[after]], axis=0)
        s = _dot(jnp.concatenate([key_rows(j), feat_ref[...]], axis=1), rhs)
        s_ref[...] = s
        return jnp.max(s, axis=0, keepdims=True)

    @pl.when((head == 0) & (qi == 0))
    def _():
        rel = (lax.broadcasted_iota(jnp.int32, (tk, tq), 0)
               - lax.broadcasted_iota(jnp.int32, (tk, tq), 1)).astype(F32)
        rel_ref[...] = jnp.abs(rel) if tq == tk else rel

    def scores_diag(s_ref):
        s = _dot(key_rows(diag), qz_ref[...])
        if tq == tk:
            dist = rel_ref[...]
        else:
            dist = jnp.abs(rel_ref[...] + (diag * tk - q0).astype(F32))
        bias = c * dist
        s = s - jnp.concatenate([bias, bias], axis=1)
        s_ref[...] = s
        return jnp.max(s, axis=0, keepdims=True)

    def chunk_offset(i):
        j, _ = chunk_of(i)
        return c * jnp.abs(j * tk - q0).astype(F32)

    def exps(s_ref, p_ref, cmax, off, m):
        m_new = jnp.maximum(m, cmax - off)
        shift = m_new + off
        for r in range(0, tk, EXP_STRIP):
            p_ref[r:r + EXP_STRIP, :] = jnp.exp2((s_ref[r:r + EXP_STRIP, :] - shift).astype(BF16))
        return m_new, jnp.exp2(m - m_new)

    def accumulate(p_ref, alpha, j):
        vtc = vt_ref[:, pl.ds(pl.multiple_of(j * tk, tk), tk)]
        acc1_ref[...] = alpha[:, :tq] * acc1_ref[...] + _dot(vtc, p_ref[:, :tq])
        acc2_ref[...] = alpha[:, tq:] * acc2_ref[...] + _dot(vtc, p_ref[:, tq:])

    def accumulate_visit(p_ref, alpha, i):
        accumulate(p_ref, alpha, jnp.where(i == 0, diag, chunk_of(i)[0]))

    cmax_diag = scores_diag(s_bufs[0])
    cmax = scores(1, s_bufs[1])
    m = jnp.minimum(cmax, NEG_BIG)
    m, alpha = exps(s_bufs[0], p_bufs[0], cmax_diag, 0.0, m)

    def body(t, carry):
        m, cmax, alpha = carry
        for parity in range(2):
            i = 2 * t + parity
            cmax_next = scores(i + 2, s_bufs[parity])
            m, alpha_next = exps(s_bufs[1 - parity], p_bufs[1 - parity], cmax, chunk_offset(i + 1), m)
            accumulate_visit(p_bufs[parity], alpha, i)
            cmax, alpha = cmax_next, alpha_next
        return m, cmax, alpha

    m, cmax, alpha = lax.fori_loop(0, (n_chunks - 2) // 2, body, (m, cmax, alpha))
    last = n_chunks - 1
    m, alpha_last = exps(s_bufs[1], p_bufs[1], cmax, chunk_offset(last), m)
    accumulate_visit(p_bufs[0], alpha, last - 1)
    accumulate_visit(p_bufs[1], alpha_last, last)

    lam = (jnp.exp(jnp.sum(lq1_ref[...] * lk1_ref[...], axis=-1, keepdims=True))
           - jnp.exp(jnp.sum(lq2_ref[...] * lk2_ref[...], axis=-1, keepdims=True))
           + lam_init)
    o_t = (acc1_ref[0:V_HEAD_DIM, :] / acc1_ref[V_HEAD_DIM:V_HEAD_DIM + 1, :]
           - lam * (acc2_ref[0:V_HEAD_DIM, :] / acc2_ref[V_HEAD_DIM:V_HEAD_DIM + 1, :]))
    o = o_t.T
    o_ref[...] = (_rmsnorm_rows(o, hn_ref[...]) * (1.0 - lam_init)).astype(o_ref.dtype)


def _attention(q, k, vt, lq1, lk1, lq2, lk2, hn, *, tq, tk, lam_init):
    s = q.shape[0]
    c = jnp.asarray([2.0 ** (-8.0 * (h + 1) / N_ATTN_HEADS) * LOG2E for h in range(N_ATTN_HEADS)], F32)
    feat, qpos = _alibi_tables(tq, tk)
    lam_spec = pl.BlockSpec((1, DIFF_HEAD_DIM), lambda h, i: (0, 0))
    kern = functools.partial(_attn_kernel, tq=tq, tk=tk, seq=s, lam_init=lam_init)
    return pl.pallas_call(
        kern,
        grid=(N_ATTN_HEADS, s // tq),
        in_specs=[
            pl.BlockSpec(memory_space=pltpu.SMEM),
            pl.BlockSpec((tq, V_HEAD_DIM), lambda h, i: (i, h)),
            pl.BlockSpec((s, V_HEAD_DIM), lambda h, i: (0, h), pipeline_mode=pl.Buffered(1)),
            pl.BlockSpec((V_ROWS, s), lambda h, i: (h, 0), pipeline_mode=pl.Buffered(1)),
            pl.BlockSpec((None, tk, V_HEAD_DIM), lambda h, i: (h, 0, 0)),
            pl.BlockSpec((None, 2, V_HEAD_DIM, 2 * tq), lambda h, i: (h, 0, 0, 0)),
            lam_spec, lam_spec, lam_spec, lam_spec,
            pl.BlockSpec((1, V_HEAD_DIM), lambda h, i: (0, 0)),
        ],
        out_specs=pl.BlockSpec((tq, V_HEAD_DIM), lambda h, i: (i, h)),
        out_shape=jax.ShapeDtypeStruct((s, ATTN_WIDTH), BF16),
        scratch_shapes=[pltpu.VMEM((V_HEAD_DIM, 2 * tq), BF16),
                        pltpu.VMEM((V_ROWS, tq), F32), pltpu.VMEM((V_ROWS, tq), F32),
                        pltpu.VMEM((tk, 2 * tq), F32), pltpu.VMEM((tk, 2 * tq), F32),
                        pltpu.VMEM((tk, 2 * tq), BF16), pltpu.VMEM((tk, 2 * tq), BF16),
                        pltpu.VMEM((tk, tq), F32)],
        compiler_params=pltpu.CompilerParams(
            dimension_semantics=("arbitrary", "arbitrary"), vmem_limit_bytes=VMEM_LIMIT_BYTES),
        name="diff_attention",
    )(c, q, k, vt, feat, qpos, lq1, lk1, lq2, lk2, hn)


def _conv_kernel(prev_ref, main_ref, next_ref, w_ref, b_ref, g_ref, beta_ref, o_ref, buf_ref,
                 shift_ref, *, ts, rows):
    i = pl.program_id(0)
    last = pl.num_programs(0) - 1
    buf_ref[0:CONV_HALO, :] = jnp.where(i > 0, prev_ref[...], 0.0)
    buf_ref[CONV_HALO:CONV_HALO + ts, :] = main_ref[...]
    buf_ref[CONV_HALO + ts:2 * CONV_HALO + ts, :] = jnp.where(i < last, next_ref[...], 0.0)
    n_shift = shift_ref.shape[1]
    for phase in range(1, SUBLANES):
        shift_ref[phase - 1] = buf_ref[phase:phase + n_shift, :]
    first_tap = CONV_HALO - CONV_KERNEL // 2
    for r0 in range(0, ts, rows):
        acc = jnp.broadcast_to(b_ref[...], (rows // SUBLANES, SUBLANES, CONV_WIDTH))
        for t in range(CONV_KERNEL):
            phase = (first_tap + t) % SUBLANES
            start = r0 + (first_tap + t) - phase
            src = buf_ref if phase == 0 else shift_ref.at[phase - 1]
            window = src[start:start + rows, :].reshape(rows // SUBLANES, SUBLANES, CONV_WIDTH)
            acc = acc + w_ref[t][None] * window
        acc = acc.reshape(rows, CONV_WIDTH)
        mu = jnp.mean(acc, axis=-1, keepdims=True)
        xc = acc - mu
        y = xc * lax.rsqrt(jnp.mean(xc * xc, axis=-1, keepdims=True) + EPS)
        y = y * g_ref[...] + beta_ref[...]
        o_ref[r0:r0 + rows, :] = (y * jax.nn.sigmoid(y)).astype(o_ref.dtype)


def _conformer_conv(u, w, b, g, beta, *, ts, rows):
    s = u.shape[0]
    nh = ts // CONV_HALO
    n_halo_blocks = s // CONV_HALO
    const = lambda i: (0, 0)
    kern = functools.partial(_conv_kernel, ts=ts, rows=rows)
    return pl.pallas_call(
        kern,
        grid=(s // ts,),
        in_specs=[
            pl.BlockSpec((CONV_HALO, CONV_WIDTH), lambda i: (jnp.maximum(i * nh - 1, 0), 0)),
            pl.BlockSpec((ts, CONV_WIDTH), lambda i: (i, 0)),
            pl.BlockSpec((CONV_HALO, CONV_WIDTH),
                         lambda i: (jnp.minimum((i + 1) * nh, n_halo_blocks - 1), 0)),
            pl.BlockSpec((CONV_KERNEL, SUBLANES, CONV_WIDTH), lambda i: (0, 0, 0)),
            pl.BlockSpec((1, CONV_WIDTH), const),
            pl.BlockSpec((1, CONV_WIDTH), const),
            pl.BlockSpec((1, CONV_WIDTH), const),
        ],
        out_specs=pl.BlockSpec((ts, CONV_WIDTH), lambda i: (i, 0)),
        out_shape=jax.ShapeDtypeStruct((s, CONV_WIDTH), BF16),
        scratch_shapes=[pltpu.VMEM((ts + 2 * CONV_HALO, CONV_WIDTH), F32),
                        pltpu.VMEM((SUBLANES - 1, ts + 2 * CONV_HALO - SUBLANES, CONV_WIDTH), F32)],
        compiler_params=pltpu.CompilerParams(
            dimension_semantics=("arbitrary",), vmem_limit_bytes=VMEM_LIMIT_BYTES),
        name="conformer_conv",
    )(u, u, u, w, b, g, beta)


FFN_CHUNKS = ((0, 1024), (1024, 2048), (2048, FFN_HIDDEN))


def _tail_kernel(x_ref, a_ref, c_ref, p_ref, woa_ref, woc_ref, fg_ref, wg_ref, wu_ref, wd_ref,
                 pg_ref, wpg_ref, wpp_ref, o_ref, act_ref):
    x1 = x_ref[...] + _dot(a_ref[...], woa_ref[...]) + _dot(c_ref[...], woc_ref[...])
    h = _rmsnorm_rows(x1, fg_ref[...]).astype(BF16)
    for c0, c1 in FFN_CHUNKS:
        g = _dot(h, wg_ref[:, c0:c1])
        u = _dot(h, wu_ref[:, c0:c1])
        act_ref[:, c0:c1] = (g * jax.nn.sigmoid(g) * u).astype(BF16)
    x2 = x1 + _dot(act_ref[...], wd_ref[...])
    hp = _rmsnorm_rows(x2, pg_ref[...]).astype(BF16)
    gate = jax.nn.sigmoid(_dot(hp, wpg_ref[...]))
    o_ref[...] = x2 + gate * _dot(p_ref[...].astype(BF16), wpp_ref[...])


def _tail(x, a, c, p, woa, woc, fg, wg, wu, wd, pg, wpg, wpp, *, tm):
    s = x.shape[0]
    const = lambda i: (0, 0)

    def resident(shape):
        return pl.BlockSpec(shape, const, pipeline_mode=pl.Buffered(1))

    return pl.pallas_call(
        _tail_kernel,
        grid=(s // tm,),
        in_specs=[
            pl.BlockSpec((tm, D_MODEL), lambda i: (i, 0)),
            pl.BlockSpec((tm, ATTN_WIDTH), lambda i: (i, 0)),
            pl.BlockSpec((tm, CONV_WIDTH), lambda i: (i, 0)),
            pl.BlockSpec((tm, PLE_DIM), lambda i: (i, 0)),
            resident((ATTN_WIDTH, D_MODEL)),
            resident((CONV_WIDTH, D_MODEL)),
            resident((1, D_MODEL)),
            resident((D_MODEL, FFN_HIDDEN)),
            resident((D_MODEL, FFN_HIDDEN)),
            resident((FFN_HIDDEN, D_MODEL)),
            resident((1, D_MODEL)),
            resident((D_MODEL, D_MODEL)),
            resident((PLE_DIM, D_MODEL)),
        ],
        out_specs=pl.BlockSpec((tm, D_MODEL), lambda i: (i, 0)),
        out_shape=jax.ShapeDtypeStruct((s, D_MODEL), F32),
        scratch_shapes=[pltpu.VMEM((tm, FFN_HIDDEN), BF16)],
        compiler_params=pltpu.CompilerParams(
            dimension_semantics=("arbitrary",), vmem_limit_bytes=VMEM_LIMIT_BYTES),
        name="tail",
    )(x, a, c, p, woa, woc, fg, wg, wu, wd, pg, wpg, wpp)


def _layer(x, p, lam_init, attn_norm, w_in, q_norm, k_norm, lq1, lk1, lq2, lk2, head_norm,
           conv_w, conv_b, conv_ln_g, conv_ln_b, w_out, ffn_norm, w_gate, w_up, w_down,
           ple_norm, w_ple_gate, w_ple_proj):
    row = lambda v: v.reshape(1, -1).astype(F32)
    a = ATTN_WIDTH
    wqk = w_in[:, :2 * a].astype(BF16)
    wvt = w_in[:, 2 * a:3 * a].T.astype(BF16)
    wc = w_in[:, 3 * a:].astype(BF16)
    n_seg = ATTN_WIDTH // DIFF_HEAD_DIM
    seg = jnp.asarray(np.kron(np.eye(n_seg), np.full((DIFF_HEAD_DIM, DIFF_HEAD_DIM),
                                                      1.0 / DIFF_HEAD_DIM)), BF16)
    qn = row(jnp.tile(q_norm, n_seg))
    kn = row(jnp.tile(k_norm, n_seg))
    q, k, vt, u = _inproj(x, row(attn_norm), wqk, wvt, wc, qn, kn, seg, tm=2 * ROW_TILE)

    attn = _attention(q, k, vt, row(lq1), row(lk1), row(lq2), row(lk2), row(head_norm),
                      tq=ATTN_Q_TILE, tk=ATTN_K_CHUNK, lam_init=lam_init)
    conv_taps = jnp.broadcast_to(conv_w.astype(F32)[:, None, :], (CONV_KERNEL, SUBLANES, CONV_WIDTH))
    conv = _conformer_conv(u, conv_taps, row(conv_b), row(conv_ln_g), row(conv_ln_b),
                           ts=2 * ROW_TILE, rows=CONV_ROW_BLOCK)
    return _tail(x, attn, conv, p, w_out[:a].astype(BF16), w_out[a:].astype(BF16), row(ffn_norm),
                 w_gate.astype(BF16), w_up.astype(BF16), w_down.astype(BF16), row(ple_norm),
                 w_ple_gate.astype(BF16), w_ple_proj.astype(BF16), tm=ROW_TILE)


def kernel(x, p, attn_norm, w_in, q_norm, k_norm, lambda_q1, lambda_k1, lambda_q2, lambda_k2,
           head_norm, conv_w, conv_b, conv_ln_g, conv_ln_b, w_out, ffn_norm, w_gate, w_up, w_down,
           ple_norm, w_ple_gate, w_ple_proj):
    depth = w_in.shape[0]
    batch = x.shape[0]
    outs = []
    for b in range(batch):
        xb = x[b]
        for i in range(depth):
            lam_init = 0.8 - 0.6 * math.exp(-0.3 * i)
            xb = _layer(xb, p[i, b], lam_init, attn_norm[i], w_in[i], q_norm[i], k_norm[i],
                        lambda_q1[i], lambda_k1[i], lambda_q2[i], lambda_k2[i], head_norm[i],
                        conv_w[i], conv_b[i], conv_ln_g[i], conv_ln_b[i], w_out[i], ffn_norm[i],
                        w_gate[i], w_up[i], w_down[i], ple_norm[i], w_ple_gate[i], w_ple_proj[i])
        outs.append(xb)
    return jnp.stack(outs, axis=0)
```
